```python
import math
import jax
import jax.numpy as jnp
from jax import lax
import numpy as np

D_MODEL = 2048
BATCH = 16
SEQ = 256
DEPTH = 4
DEC_BATCH = 2
DEC_SEQ = 2048
PAST_LEN = 256

GRID_W = 64
HEAD_DIM = 64
GROUP_WIDTH = D_MODEL // 4
MIX_WIDTH = 4 * GROUP_WIDTH
NA_HEADS = GROUP_WIDTH // HEAD_DIM
NA_WIN_R = 8
NA_WIN_C = 16
NA_COL_BLOCK = 16
NA_COL_BAND = NA_COL_BLOCK + NA_WIN_C
CONV_CH = GROUP_WIDTH
CONV_K = 3
SWA_HEADS = GROUP_WIDTH // HEAD_DIM
SWA_KV_HEADS = 2
SWA_GROUP = SWA_HEADS // SWA_KV_HEADS
SWA_WINDOW = 128
SWA_BLOCK = 128
DIFF_HEADS = GROUP_WIDTH // (2 * HEAD_DIM)
DIFF_V = 2 * HEAD_DIM
DIFF_BLOCK = 128
MLP_HIDDEN = 4 * D_MODEL
N_MOD = 6
ROPE_BASE = 10000.0
RMS_EPS = 1e-6
NEG_INF = -1e30
SCALE = HEAD_DIM ** -0.5
IN_SIZES = (GROUP_WIDTH, GROUP_WIDTH, GROUP_WIDTH,
            CONV_CH, CONV_CH, CONV_CH,
            SWA_HEADS * HEAD_DIM, SWA_KV_HEADS * HEAD_DIM, SWA_KV_HEADS * HEAD_DIM,
            DIFF_HEADS * 2 * HEAD_DIM, DIFF_HEADS * 2 * HEAD_DIM, DIFF_HEADS * DIFF_V)
IN_COLS = sum(IN_SIZES)

kernel_name = 'hybrid_diffusion_prefix_trunk'


def rmsnorm(x, g):
    xf = x.astype(jnp.float32)
    y = xf * lax.rsqrt(jnp.mean(xf * xf, axis=-1, keepdims=True) + RMS_EPS)
    return (y * g.astype(jnp.float32)).astype(x.dtype)


def heads(t, h):
    return t.reshape(t.shape[:-1] + (h, t.shape[-1] // h))


def split_proj(p):
    out, off = [], 0
    for n in IN_SIZES:
        out.append(p[..., off:off + n])
        off += n
    return out


def modulation(cvec, w, b):
    m = jax.nn.silu(cvec) @ w + b
    return m.reshape(m.shape[:-1] + (N_MOD, D_MODEL))


def mod_term(m, i):
    return m[..., i, :][..., None, :]


def pre_mix(x, m, g):
    return rmsnorm(x, g[0]) * (1 + mod_term(m, 1)) + mod_term(m, 0)


def post_layer(x, mix, m, g, w_out, w1, w2):
    x = x + mod_term(m, 2) * rmsnorm(mix @ w_out, g[1])
    h = rmsnorm(x, g[2]) * (1 + mod_term(m, 4)) + mod_term(m, 3)
    f = jnp.square(jax.nn.relu(h @ w1)) @ w2
    return x + mod_term(m, 5) * rmsnorm(f, g[3])


def axial_angles(T):
    t = jnp.arange(T)
    rows = (t // GRID_W).astype(jnp.float32)
    cols = (t % GRID_W).astype(jnp.float32)
    n = HEAD_DIM // 4
    inv = ROPE_BASE ** (-jnp.arange(n, dtype=jnp.float32) / n)
    return rows[:, None] * inv[None], cols[:, None] * inv[None]


def _rotate(x, ang):
    n = ang.shape[-1]
    cos = jnp.cos(ang)[None, :, None, :].astype(x.dtype)
    sin = jnp.sin(ang)[None, :, None, :].astype(x.dtype)
    x1, x2 = x[..., :n], x[..., n:]
    return jnp.concatenate([x1 * cos - x2 * sin, x1 * sin + x2 * cos], axis=-1)


def rope_2d(x, ang_r, ang_c):
    h = x.shape[-1] // 2
    return jnp.concatenate([_rotate(x[..., :h], ang_r), _rotate(x[..., h:], ang_c)], axis=-1)


def conv_mixer(u, gb, gc, w):
    z = gc * u
    T = z.shape[1]
    zp = jnp.pad(z, ((0, 0), (1, 1), (0, 0)))
    conv = zp[:, :T] * w[0] + zp[:, 1:T + 1] * w[1] + zp[:, 2:] * w[2]
    return gb * conv


def sink_softmax(s, sink):
    m = jnp.maximum(jnp.max(s, axis=-1, keepdims=True), sink)
    e = jnp.exp(s - m)
    return e / (jnp.sum(e, axis=-1, keepdims=True) + jnp.exp(sink - m))


def ctx_mha(q, k, v):
    s = jnp.einsum('bqhd,bkhd->bhqk', q, k).astype(jnp.float32) * SCALE
    p = jax.nn.softmax(s, axis=-1).astype(v.dtype)
    o = jnp.einsum('bhqk,bkhd->bqhd', p, v)
    return o.reshape(o.shape[:2] + (-1,))


def ctx_gqa_sink(q, k, v, sink):
    B, S = q.shape[:2]
    qg = q.reshape(B, S, SWA_KV_HEADS, SWA_GROUP, HEAD_DIM)
    s = jnp.einsum('bqngd,bknd->bngqk', qg, k).astype(jnp.float32) * SCALE
    sk = sink.astype(jnp.float32).reshape(SWA_KV_HEADS, SWA_GROUP)[None, :, :, None, None]
    p = sink_softmax(s, sk).astype(v.dtype)
    o = jnp.einsum('bngqk,bknd->bqngd', p, v)
    return o.reshape(B, S, SWA_HEADS * HEAD_DIM)


def lambda_value(lam_p, lam_init):
    lp = lam_p.astype(jnp.float32)
    return jnp.exp(jnp.sum(lp[0] * lp[1])) - jnp.exp(jnp.sum(lp[2] * lp[3])) + lam_init


def diff_combine(q1, q2, k1, k2, v, lam):
    a1 = jax.nn.softmax(jnp.einsum('bqhd,bkhd->bhqk', q1, k1).astype(jnp.float32) * SCALE, axis=-1)
    a2 = jax.nn.softmax(jnp.einsum('bqhd,bkhd->bhqk', q2, k2).astype(jnp.float32) * SCALE, axis=-1)
    return jnp.einsum('bhqk,bkhe->bqhe', (a1 - lam * a2).astype(v.dtype), v)


def diff_finish(o, g, lam_init):
    o = rmsnorm(o, g) * (1.0 - lam_init)
    return o.reshape(o.shape[:2] + (DIFF_HEADS * DIFF_V,))


def neighbourhood_attention(q, k, v, k_ctx, v_ctx, rpb):
    B, T, H, d = q.shape
    rows = T // GRID_W
    kr = min(NA_WIN_R, rows)
    ncb = GRID_W // NA_COL_BLOCK
    r = jnp.arange(rows)
    row_idx = jnp.clip(r - kr // 2, 0, rows - kr)[:, None] + jnp.arange(kr)[None]
    cb = jnp.arange(ncb)
    band = jnp.clip(cb * NA_COL_BLOCK - NA_WIN_C // 2, 0, GRID_W - NA_COL_BAND)
    col_idx = band[:, None] + jnp.arange(NA_COL_BAND)[None]
    qc = cb[:, None] * NA_COL_BLOCK + jnp.arange(NA_COL_BLOCK)[None]
    cs = jnp.clip(qc - NA_WIN_C // 2, 0, GRID_W - NA_WIN_C)
    kc = col_idx[:, None, :]
    col_ok = (kc >= cs[..., None]) & (kc < cs[..., None] + NA_WIN_C)
    kloc = kr * NA_COL_BAND
    mask = jnp.broadcast_to(col_ok[:, :, None, :], (ncb, NA_COL_BLOCK, kr, NA_COL_BAND)).reshape(ncb, NA_COL_BLOCK, kloc)
    ri = row_idx[:, None, :, None]
    ci = col_idx[None, :, None, :]
    k_blk = k.reshape(B, rows, GRID_W, H, d)[:, ri, ci].reshape(B, rows, ncb, kloc, H, d)
    v_blk = v.reshape(B, rows, GRID_W, H, d)[:, ri, ci].reshape(B, rows, ncb, kloc, H, d)
    q_blk = q.reshape(B, rows, ncb, NA_COL_BLOCK, H, d)
    dr = row_idx - r[:, None] + NA_WIN_R - 1
    dc = jnp.clip(kc - qc[..., None], -(NA_WIN_C - 1), NA_WIN_C - 1) + NA_WIN_C - 1
    bias = rpb[:, dr[:, None, None, :, None], dc[None, :, :, None, :]]
    bias = bias.reshape(H, rows, ncb, NA_COL_BLOCK, kloc).astype(jnp.float32)
    s_loc = jnp.einsum('brcqhd,brckhd->bhrcqk', q_blk, k_blk).astype(jnp.float32) * SCALE + bias[None]
    s_loc = jnp.where(mask[None, None, None], s_loc, NEG_INF)
    s_ctx = jnp.einsum('brcqhd,bkhd->bhrcqk', q_blk, k_ctx).astype(jnp.float32) * SCALE
    p = jax.nn.softmax(jnp.concatenate([s_loc, s_ctx], axis=-1), axis=-1).astype(v.dtype)
    o = (jnp.einsum('bhrcqk,brckhd->brcqhd', p[..., :kloc], v_blk)
         + jnp.einsum('bhrcqk,bkhd->brcqhd', p[..., kloc:], v_ctx))
    return o.reshape(B, T, H * d)


def window_gqa_sink(q, k, v, k_ctx, v_ctx, sink):
    B, T, H, d = q.shape
    nb = T // SWA_BLOCK
    bw = SWA_BLOCK + 2 * SWA_WINDOW
    pad = ((0, 0), (SWA_WINDOW, SWA_WINDOW), (0, 0), (0, 0))
    idx = jnp.arange(nb)[:, None] * SWA_BLOCK + jnp.arange(bw)[None]
    k_blk = jnp.pad(k, pad)[:, idx]
    v_blk = jnp.pad(v, pad)[:, idx]
    qpos = jnp.arange(nb)[:, None] * SWA_BLOCK + jnp.arange(SWA_BLOCK)[None]
    kpos = idx - SWA_WINDOW
    valid = ((jnp.abs(qpos[:, :, None] - kpos[:, None, :]) <= SWA_WINDOW)
             & (kpos[:, None, :] >= 0) & (kpos[:, None, :] < T))
    qb = q.reshape(B, nb, SWA_BLOCK, SWA_KV_HEADS, SWA_GROUP, d)
    s_loc = jnp.einsum('bnqhgd,bnjhd->bhgnqj', qb, k_blk).astype(jnp.float32) * SCALE
    s_loc = jnp.where(valid[None, None, None], s_loc, NEG_INF)
    s_ctx = jnp.einsum('bnqhgd,bjhd->bhgnqj', qb, k_ctx).astype(jnp.float32) * SCALE
    sk = sink.astype(jnp.float32).reshape(SWA_KV_HEADS, SWA_GROUP)[None, :, :, None, None, None]
    p = sink_softmax(jnp.concatenate([s_loc, s_ctx], axis=-1), sk).astype(v.dtype)
    o = (jnp.einsum('bhgnqj,bnjhd->bnqhgd', p[..., :bw], v_blk)
         + jnp.einsum('bhgnqj,bjhd->bnqhgd', p[..., bw:], v_ctx))
    return o.reshape(B, T, H * d)


def diff_attention_latent(q1, q2, k1, k2, v, k1c, k2c, vc, lam):
    B, T, H, d = q1.shape
    nb = T // DIFF_BLOCK
    K1 = jnp.concatenate([k1, k1c], axis=1)
    K2 = jnp.concatenate([k2, k2c], axis=1)
    V = jnp.concatenate([v, vc], axis=1)
    qb = jnp.stack([q1, q2]).reshape(2, B, nb, DIFF_BLOCK, H, d).transpose(2, 0, 1, 3, 4, 5)
    o = lax.map(lambda qs: diff_combine(qs[0], qs[1], K1, K2, V, lam), qb)
    return o.transpose(1, 0, 2, 3, 4).reshape(B, T, H, DIFF_V)


def context_mixer(p, conv_w, sink, lam, lam_init, dg):
    B, S, _ = p.shape
    na_q, na_k, na_v, u, gb, gc, sq, sk, sv, dq, dk, dv = split_proj(p)
    na_q, na_k, na_v = heads(na_q, NA_HEADS), heads(na_k, NA_HEADS), heads(na_v, NA_HEADS)
    o_a = ctx_mha(na_q, na_k, na_v)
    o_b = conv_mixer(u, gb, gc, conv_w)
    sq, sk, sv = heads(sq, SWA_HEADS), heads(sk, SWA_KV_HEADS), heads(sv, SWA_KV_HEADS)
    o_c = ctx_gqa_sink(sq, sk, sv, sink)
    dq, dk, dv = heads(dq, DIFF_HEADS), heads(dk, DIFF_HEADS), heads(dv, DIFF_HEADS)
    o_d = diff_finish(diff_combine(dq[..., :HEAD_DIM], dq[..., HEAD_DIM:], dk[..., :HEAD_DIM], dk[..., HEAD_DIM:], dv, lam), dg, lam_init)
    mix = jnp.concatenate([o_a, o_b, o_c, o_d], axis=-1)
    return mix, jnp.stack([na_k, na_v], axis=1), jnp.stack([sk, sv], axis=1), jnp.stack([dk, dv], axis=1)


def latent_mixer(p, na_kv, swa_kv, diff_kv, conv_w, rpb, sink, lam, lam_init, dg, ang_r, ang_c):
    na_q, na_k, na_v, u, gb, gc, sq, sk, sv, dq, dk, dv = split_proj(p)
    na_q, na_k, na_v = heads(na_q, NA_HEADS), heads(na_k, NA_HEADS), heads(na_v, NA_HEADS)
    o_a = neighbourhood_attention(na_q, na_k, na_v, na_kv[:, 0], na_kv[:, 1], rpb)
    o_b = conv_mixer(u, gb, gc, conv_w)
    sq = rope_2d(heads(sq, SWA_HEADS), ang_r, ang_c)
    sk = rope_2d(heads(sk, SWA_KV_HEADS), ang_r, ang_c)
    o_c = window_gqa_sink(sq, sk, heads(sv, SWA_KV_HEADS), swa_kv[:, 0], swa_kv[:, 1], sink)
    dq, dk, dv = heads(dq, DIFF_HEADS), heads(dk, DIFF_HEADS), heads(dv, DIFF_HEADS)
    q1 = rope_2d(dq[..., :HEAD_DIM], ang_r, ang_c)
    q2 = rope_2d(dq[..., HEAD_DIM:], ang_r, ang_c)
    k1 = rope_2d(dk[..., :HEAD_DIM], ang_r, ang_c)
    k2 = rope_2d(dk[..., HEAD_DIM:], ang_r, ang_c)
    kc = diff_kv[:, 0]
    o_d = diff_finish(diff_attention_latent(q1, q2, k1, k2, dv, kc[..., :HEAD_DIM], kc[..., HEAD_DIM:], diff_kv[:, 1], lam), dg, lam_init)
    return jnp.concatenate([o_a, o_b, o_c, o_d], axis=-1)


def setup_inputs(seed: int = 0) -> dict:
    key = jax.random.key(seed)
    ks = jax.random.split(key, 20)
    nrm = jax.random.normal
    f32 = jnp.float32
    return {
        'x_prompt': nrm(ks[0], (BATCH, SEQ, D_MODEL), f32),
        'x_sample': nrm(ks[1], (DEC_BATCH, DEC_SEQ, D_MODEL), f32),
        'cache_na_kv': nrm(ks[2], (DEC_BATCH, DEPTH, 2, PAST_LEN, NA_HEADS, HEAD_DIM), f32),
        'cache_swa_kv': nrm(ks[3], (DEC_BATCH, DEPTH, 2, PAST_LEN, SWA_KV_HEADS, HEAD_DIM), f32),
        'cache_diff_kv': nrm(ks[4], (DEC_BATCH, DEPTH, 2, PAST_LEN, DIFF_HEADS, DIFF_V), f32),
        'c': nrm(ks[5], (DEC_BATCH, D_MODEL), f32),
        'c_ctx': nrm(ks[6], (D_MODEL,), f32),
        'ada_w': nrm(ks[7], (DEPTH, D_MODEL, N_MOD * D_MODEL), f32) * (0.5 * D_MODEL ** -0.5),
        'ada_b': nrm(ks[8], (DEPTH, N_MOD * D_MODEL), f32) * 0.01,
        'norm_g': 1.0 + 0.05 * nrm(ks[9], (DEPTH, 4, D_MODEL), f32),
        'w_in': nrm(ks[10], (DEPTH, D_MODEL, IN_COLS), f32) * D_MODEL ** -0.5,
        'conv_w': nrm(ks[11], (DEPTH, CONV_K, CONV_CH), f32) * CONV_K ** -0.5,
        'na_rpb': nrm(ks[12], (DEPTH, NA_HEADS, 2 * NA_WIN_R - 1, 2 * NA_WIN_C - 1), f32) * 0.1,
        'swa_sink': nrm(ks[13], (DEPTH, SWA_HEADS), f32),
        'diff_lambda': nrm(ks[14], (DEPTH, 4, HEAD_DIM), f32) * 0.1,
        'diff_norm_g': 1.0 + 0.05 * nrm(ks[15], (DEPTH, DIFF_V), f32),
        'w_out': nrm(ks[16], (DEPTH, MIX_WIDTH, D_MODEL), f32) * MIX_WIDTH ** -0.5,
        'mlp_w1': nrm(ks[17], (DEPTH, D_MODEL, MLP_HIDDEN), f32) * D_MODEL ** -0.5,
        'mlp_w2': nrm(ks[18], (DEPTH, MLP_HIDDEN, D_MODEL), f32) * MLP_HIDDEN ** -0.5,
    }


def reference(x_prompt, x_sample, cache_na_kv, cache_swa_kv, cache_diff_kv, c, c_ctx,
              ada_w, ada_b, norm_g, w_in, conv_w, na_rpb, swa_sink, diff_lambda, diff_norm_g,
              w_out, mlp_w1, mlp_w2):
    ang_r, ang_c = axial_angles(x_sample.shape[1])
    yp, ys = x_prompt, x_sample
    na_st, swa_st, diff_st = [], [], []
    for l in range(DEPTH):
        lam_init = 0.8 - 0.6 * math.exp(-0.3 * l)
        lam = lambda_value(diff_lambda[l], lam_init)
        mod_p = modulation(c_ctx, ada_w[l], ada_b[l])
        h = pre_mix(yp, mod_p, norm_g[l])
        mix, na_kv, swa_kv, d_kv = context_mixer(h @ w_in[l], conv_w[l], swa_sink[l], lam, lam_init, diff_norm_g[l])
        yp = post_layer(yp, mix, mod_p, norm_g[l], w_out[l], mlp_w1[l], mlp_w2[l])
        na_st.append(na_kv)
        swa_st.append(swa_kv)
        diff_st.append(d_kv)
        mod_s = modulation(c, ada_w[l], ada_b[l])
        h = pre_mix(ys, mod_s, norm_g[l])
        mix = latent_mixer(h @ w_in[l], cache_na_kv[:, l], cache_swa_kv[:, l], cache_diff_kv[:, l], conv_w[l],
                           na_rpb[l], swa_sink[l], lam, lam_init, diff_norm_g[l], ang_r, ang_c)
        ys = post_layer(ys, mix, mod_s, norm_g[l], w_out[l], mlp_w1[l], mlp_w2[l])
    return (yp, ys, jnp.stack(na_st, axis=1), jnp.stack(swa_st, axis=1), jnp.stack(diff_st, axis=1))
```

```python
import functools
import math

import numpy as np
import jax
import jax.numpy as jnp
from jax import lax
from jax.experimental import pallas as pl
from jax.experimental.pallas import tpu as pltpu

F32 = jnp.float32
BF16 = jnp.bfloat16

D = 2048
N_CTX_B, CTX_S = 16, 256
N_LAT_B, LAT_T = 2, 2048
CTX_ROWS = N_CTX_B * CTX_S
LAT_ROWS = N_LAT_B * LAT_T
ROWS = CTX_ROWS + LAT_ROWS
DEPTH = 4
GRID_W = 64
GRID_H = LAT_T // GRID_W
HD = 64
GW = 512
IN_COLS = 5376
HIDDEN = 4 * D
N_MOD = 6
EPS = 1e-6
NEG = -1e30
SCALE = HD ** -0.5
NA_R, NA_C = 8, 16
SWA_W = 128
LANES = 128

C_NAQ, C_NAK, C_NAV = 0, 512, 1024
C_U, C_GB, C_GC = 1536, 2048, 2560
C_SQ, C_SK, C_SV = 3072, 3584, 3712
C_DQ, C_DK, C_DV = 3840, 4352, 4864

VMEM_LIMIT = 56 * 1024 * 1024


def _cp(sem):
    return pltpu.CompilerParams(dimension_semantics=sem, vmem_limit_bytes=VMEM_LIMIT)


def _rms(x):
    return x * lax.rsqrt(jnp.mean(x * x, axis=-1, keepdims=True) + EPS)


def _qk(q, k):
    return lax.dot_general(q, k, (((1,), (1,)), ((), ())), preferred_element_type=F32)


def _pv(p, v):
    return jnp.dot(p, v, preferred_element_type=F32)


def _lane_lo(rows):
    return lax.broadcasted_iota(jnp.int32, (rows, LANES), 1) < HD


def _row_group(i, tm):
    return jnp.maximum(i * tm - LAT_T, 0) // LAT_T


def _mod_kernel(c_ref, w_ref, b_ref, o_ref):
    cv = c_ref[...]
    s = cv * (1.0 / (1.0 + jnp.exp(-cv)))
    o_ref[...] = jnp.dot(s.astype(BF16), w_ref[...].astype(BF16),
                         preferred_element_type=F32) + b_ref[...]


def _modulation(cvec, ada_w, ada_b):
    tn = 1024
    n = N_MOD * D
    return pl.pallas_call(
        _mod_kernel,
        grid=(DEPTH, n // tn),
        in_specs=[
            pl.BlockSpec((8, D), lambda l, j: (0, 0)),
            pl.BlockSpec((None, D, tn), lambda l, j: (l, 0, j)),
            pl.BlockSpec((None, 1, tn), lambda l, j: (l, 0, j)),
        ],
        out_specs=pl.BlockSpec((None, 8, tn), lambda l, j: (l, 0, j)),
        out_shape=jax.ShapeDtypeStruct((DEPTH, 8, n), F32),
        compiler_params=_cp(("parallel", "parallel")),
        name="modulation",
    )(cvec, ada_w, ada_b.reshape(DEPTH, 1, n))


def _inproj_kernel(x_ref, g_ref, m_ref, w_ref, o_ref, h_ref):
    @pl.when(pl.program_id(1) == 0)
    def _():
        h = _rms(x_ref[...]) * g_ref[0:1, :]
        h = h * (1.0 + m_ref[1:2, :]) + m_ref[0:1, :]
        h_ref[...] = h.astype(BF16)

    o_ref[...] = jnp.dot(h_ref[...], w_ref[...], preferred_element_type=F32)


def _inproj(l, x, norm_g, mods, w_in):
    tm, tn = 1024, 768
    return pl.pallas_call(
        _inproj_kernel,
        grid=(ROWS // tm, IN_COLS // tn),
        in_specs=[
            pl.BlockSpec((tm, D), lambda i, j: (i, 0)),
            pl.BlockSpec((None, 4, D), lambda i, j: (l, 0, 0)),
            pl.BlockSpec((None, None, N_MOD, D), lambda i, j: (l, _row_group(i, tm), 0, 0)),
            pl.BlockSpec((None, D, tn), lambda i, j: (l, 0, j)),
        ],
        out_specs=pl.BlockSpec((tm, tn), lambda i, j: (i, j)),
        out_shape=jax.ShapeDtypeStruct((ROWS, IN_COLS), F32),
        scratch_shapes=[pltpu.VMEM((tm, D), BF16)],
        compiler_params=_cp(("parallel", "arbitrary")),
        name="inproj",
    )(x, norm_g, mods, w_in)


def _conv3(u, gb, gc, w):
    z = gc * u
    n = z.shape[0]
    row = lax.broadcasted_iota(jnp.int32, z.shape, 0)
    zp = jnp.where(row == 0, 0.0, pltpu.roll(z, 1, 0))
    zn = jnp.where(row == n - 1, 0.0, pltpu.roll(z, n - 1, 0))
    return gb * (zp * w[0:1, :] + z * w[1:2, :] + zn * w[2:3, :])


def _lambda(lam_ref, lam_init):
    lp = lam_ref[...]
    a = jnp.sum(lp[0:1, :] * lp[1:2, :], axis=-1, keepdims=True)
    b = jnp.sum(lp[2:3, :] * lp[3:4, :], axis=-1, keepdims=True)
    return jnp.exp(a) - jnp.exp(b) + lam_init


def _softmax(s):
    e = jnp.exp(s - jnp.max(s, axis=-1, keepdims=True))
    return e / jnp.sum(e, axis=-1, keepdims=True)


def _diff_head(q, kb, vb, lam, dg, lam_init, lo):
    a1 = _softmax(_qk(jnp.where(lo, q, 0.0).astype(BF16), kb))
    a2 = _softmax(_qk(jnp.where(lo, 0.0, q).astype(BF16), kb))
    o = _pv((a1 - lam * a2).astype(BF16), vb)
    return (_rms(o) * dg) * (1.0 - lam_init)


def _double_head(x, own):
    return jnp.where(own, x, pltpu.roll(x, HD, 1))


def _pick_row(tab, idx):
    row = lax.broadcasted_iota(jnp.int32, tab.shape, 0)
    return jnp.sum(jnp.where(row == idx, tab, 0.0), axis=0, keepdims=True)[:, 0:1]


def _swap16(x):
    lane = lax.broadcasted_iota(jnp.int32, x.shape, 1)
    return jnp.where(lane % 32 < 16, pltpu.roll(x, LANES - 16, 1), pltpu.roll(x, 16, 1))


def _rope(x, cos, sin):
    return x * cos + _swap16(x) * sin


def _ctx_kernel(p_ref, cw_ref, sink_ref, lam_ref, dg_ref, o_ref, *, lam_init):
    s_len = CTX_S
    lo = _lane_lo(s_len)
    lane_hi = jnp.logical_not(lo)

    def blk(c):
        return p_ref[:, c:c + LANES]

    for hp in range(GW // LANES):
        q = blk(C_NAQ + hp * LANES) * SCALE
        kb = blk(C_NAK + hp * LANES).astype(BF16)
        vb = blk(C_NAV + hp * LANES).astype(BF16)
        outs = []
        for half in range(2):
            qm = jnp.where(lo if half == 0 else lane_hi, q, 0.0).astype(BF16)
            s = _qk(qm, kb)
            e = jnp.exp(s - jnp.max(s, axis=-1, keepdims=True))
            outs.append(_pv(e.astype(BF16), vb) / jnp.sum(e, axis=-1, keepdims=True))
        o_ref[:, hp * LANES:(hp + 1) * LANES] = jnp.where(lo, outs[0], outs[1]).astype(BF16)

    for cb in range(GW // LANES):
        c = cb * LANES
        y = _conv3(blk(C_U + c), blk(C_GB + c), blk(C_GC + c), cw_ref[:, c:c + LANES])
        o_ref[:, GW + c:GW + c + LANES] = y.astype(BF16)

    k2 = blk(C_SK)
    v2 = blk(C_SV)
    sink_tab = sink_ref[...]
    for n in range(2):
        own = lo if n == 0 else lane_hi
        kd = _double_head(k2, own).astype(BF16)
        vd = _double_head(v2, own).astype(BF16)
        for c in range(2):
            q = blk(C_SQ + (2 * n + c) * LANES) * SCALE
            outs = []
            for half in range(2):
                h = 4 * n + 2 * c + half
                sink = _pick_row(sink_tab, h)
                qm = jnp.where(lo if half == 0 else lane_hi, q, 0.0).astype(BF16)
                s = _qk(qm, kd)
                m = jnp.maximum(jnp.max(s, axis=-1, keepdims=True), sink)
                e = jnp.exp(s - m)
                den = jnp.sum(e, axis=-1, keepdims=True) + jnp.exp(sink - m)
                outs.append(_pv(e.astype(BF16), vd) / den)
            col = 2 * GW + (2 * n + c) * LANES
            o_ref[:, col:col + LANES] = jnp.where(lo, outs[0], outs[1]).astype(BF16)

    lam = _lambda(lam_ref, lam_init)
    dg = dg_ref[...]
    for h in range(GW // LANES):
        q = blk(C_DQ + h * LANES) * SCALE
        kb = blk(C_DK + h * LANES).astype(BF16)
        vb = blk(C_DV + h * LANES).astype(BF16)
        o = _diff_head(q, kb, vb, lam, dg, lam_init, lo)
        col = 3 * GW + h * LANES
        o_ref[:, col:col + LANES] = o.astype(BF16)


def _ctx_mixer(l, p, conv_w, sink_tab, diff_lambda, diff_g, lam_init):
    return pl.pallas_call(
        functools.partial(_ctx_kernel, lam_init=lam_init),
        grid=(N_CTX_B,),
        in_specs=[
            pl.BlockSpec((CTX_S, IN_COLS), lambda b: (b, 0)),
            pl.BlockSpec((None, 3, GW), lambda b: (l, 0, 0)),
            pl.BlockSpec((None, 8, LANES), lambda b: (l, 0, 0)),
            pl.BlockSpec((None, 4, HD), lambda b: (l, 0, 0)),
            pl.BlockSpec((None, 1, LANES), lambda b: (l, 0, 0)),
        ],
        out_specs=pl.BlockSpec((CTX_S, D), lambda b: (b, 0)),
        out_shape=jax.ShapeDtypeStruct((CTX_ROWS, D), BF16),
        compiler_params=_cp(("parallel",)),
        name="ctx_mixer",
    )(p, conv_w, sink_tab, diff_lambda, diff_g)


def _na_kernel(q_ref, k_ref, v_ref, kc_ref, vc_ref, bt_ref, o_ref, kb_ref, vb_ref):
    kb_ref[...] = k_ref[...].astype(BF16)
    vb_ref[...] = v_ref[...].astype(BF16)
    kcb = kc_ref[...].astype(BF16)
    vcb = vc_ref[...].astype(BF16)
    lo = _lane_lo(GRID_W)
    nk = NA_R * GRID_W
    qc = lax.broadcasted_iota(jnp.int32, (GRID_W, nk), 0)
    kc = lax.broadcasted_iota(jnp.int32, (GRID_W, nk), 1) % GRID_W
    cs = jnp.clip(qc - NA_C // 2, 0, GRID_W - NA_C)
    col_ok = (kc >= cs) & (kc < cs + NA_C)

    def body(r, carry):
        rs = jnp.clip(r - NA_R // 2, 0, GRID_H - NA_R)
        d0 = rs - r + NA_R - 1
        q = q_ref[pl.ds(pl.multiple_of(r * GRID_W, GRID_W), GRID_W), :] * SCALE
        kw = kb_ref[pl.ds(pl.multiple_of(rs * GRID_W, GRID_W), nk), :]
        vw = vb_ref[pl.ds(pl.multiple_of(rs * GRID_W, GRID_W), nk), :]
        outs = []
        for half in range(2):
            qm = jnp.where(lo, q, 0.0) if half == 0 else jnp.where(lo, 0.0, q)
            qm = qm.astype(BF16)
            s_loc = jnp.where(col_ok, _qk(qm, kw) + bt_ref[half, d0], NEG)
            s_ctx = _qk(qm, kcb)
            m = jnp.maximum(jnp.max(s_loc, axis=-1, keepdims=True),
                            jnp.max(s_ctx, axis=-1, keepdims=True))
            e_loc = jnp.exp(s_loc - m)
            e_ctx = jnp.exp(s_ctx - m)
            den = jnp.sum(e_loc, axis=-1, keepdims=True) + jnp.sum(e_ctx, axis=-1, keepdims=True)
            outs.append((_pv(e_loc.astype(BF16), vw) + _pv(e_ctx.astype(BF16), vcb)) / den)
        o_ref[pl.ds(pl.multiple_of(r * GRID_W, GRID_W), GRID_W), :] = (
            jnp.where(lo, outs[0], outs[1]).astype(BF16))
        return carry

    lax.fori_loop(0, GRID_H, body, 0)


def _na_bias_table(rpb):
    qc = np.arange(GRID_W)[:, None]
    kc = np.arange(GRID_W)[None, :]
    dc = np.clip(kc - qc, -(NA_C - 1), NA_C - 1) + NA_C - 1
    t = rpb[:, :, dc]
    slabs = [t[:, d0:d0 + NA_R].transpose(0, 2, 1, 3).reshape(8, GRID_W, NA_R * GRID_W)
             for d0 in range(NA_R)]
    return jnp.stack(slabs, axis=1)


def _na(l, p, cache_na, bias_tab):
    lat_blk = CTX_ROWS // LAT_T
    cache_spec = lambda which: pl.BlockSpec(
        (None, None, None, CTX_S, LANES), lambda b, hp: (b, l, which, 0, hp))
    return pl.pallas_call(
        _na_kernel,
        grid=(N_LAT_B, GW // LANES),
        in_specs=[
            pl.BlockSpec((LAT_T, LANES), lambda b, hp: (lat_blk + b, C_NAQ // LANES + hp)),
            pl.BlockSpec((LAT_T, LANES), lambda b, hp: (lat_blk + b, C_NAK // LANES + hp)),
            pl.BlockSpec((LAT_T, LANES), lambda b, hp: (lat_blk + b, C_NAV // LANES + hp)),
            cache_spec(0),
            cache_spec(1),
            pl.BlockSpec((2, NA_R, GRID_W, NA_R * GRID_W), lambda b, hp: (hp, 0, 0, 0)),
        ],
        out_specs=pl.BlockSpec((LAT_T, LANES), lambda b, hp: (b, hp)),
        out_shape=jax.ShapeDtypeStruct((LAT_ROWS, GW), BF16),
        scratch_shapes=[pltpu.VMEM((LAT_T, LANES), BF16), pltpu.VMEM((LAT_T, LANES), BF16)],
        compiler_params=_cp(("parallel", "parallel")),
        name="na_latent",
    )(p, p, p, cache_na, cache_na, bias_tab)


def _conv_kernel(u_ref, gb_ref, gc_ref, w_ref, o_ref):
    o_ref[...] = _conv3(u_ref[...], gb_ref[...], gc_ref[...], w_ref[...]).astype(BF16)


def _conv_latent(l, p, conv_w):
    lat_blk = CTX_ROWS // LAT_T
    spec = lambda c0: pl.BlockSpec((LAT_T, LANES), lambda b, c: (lat_blk + b, c0 // LANES + c))
    return pl.pallas_call(
        _conv_kernel,
        grid=(N_LAT_B, GW // LANES),
        in_specs=[spec(C_U), spec(C_GB), spec(C_GC),
                  pl.BlockSpec((None, 3, LANES), lambda b, c: (l, 0, c))],
        out_specs=pl.BlockSpec((LAT_T, LANES), lambda b, c: (b, c)),
        out_shape=jax.ShapeDtypeStruct((LAT_ROWS, GW), BF16),
        compiler_params=_cp(("parallel", "parallel")),
        name="conv_latent",
    )(p, p, p, conv_w)


SWA_BLK = 128
SWA_KEYS = SWA_BLK + 2 * SWA_W
SWA_PAD = LAT_T + 2 * SWA_W


def _swa_kernel(q_ref, k_ref, v_ref, kc_ref, vc_ref, cos_ref, sin_ref, sink_ref, o_ref,
                kd_ref, vd_ref, kcd_ref, vcd_ref):
    n = pl.program_id(1)
    i = pl.program_id(2)

    @pl.when(i == 0)
    def _():
        own = jnp.logical_not(_lane_lo(LAT_T)).astype(jnp.int32) == n
        own_c = jnp.logical_not(_lane_lo(CTX_S)).astype(jnp.int32) == n
        kr = _rope(k_ref[...], cos_ref[...], sin_ref[...])
        zeros = jnp.zeros((SWA_W, LANES), BF16)
        kd_ref[0:SWA_W, :] = zeros
        kd_ref[SWA_W + LAT_T:SWA_PAD, :] = zeros
        vd_ref[0:SWA_W, :] = zeros
        vd_ref[SWA_W + LAT_T:SWA_PAD, :] = zeros
        kd_ref[SWA_W:SWA_W + LAT_T, :] = _double_head(kr, own).astype(BF16)
        vd_ref[SWA_W:SWA_W + LAT_T, :] = _double_head(v_ref[...], own).astype(BF16)
        kcd_ref[...] = _double_head(kc_ref[...], own_c).astype(BF16)
        vcd_ref[...] = _double_head(vc_ref[...], own_c).astype(BF16)

    lo = _lane_lo(SWA_BLK)
    start = pl.multiple_of(i * SWA_BLK, SWA_BLK)
    cos = cos_ref[pl.ds(start, SWA_BLK), :]
    sin = sin_ref[pl.ds(start, SWA_BLK), :]
    kw = kd_ref[pl.ds(start, SWA_KEYS), :]
    vw = vd_ref[pl.ds(start, SWA_KEYS), :]
    kcb = kcd_ref[...]
    vcb = vcd_ref[...]
    a = lax.broadcasted_iota(jnp.int32, (SWA_BLK, SWA_KEYS), 0)
    j = lax.broadcasted_iota(jnp.int32, (SWA_BLK, SWA_KEYS), 1)
    kpos = start - SWA_W + j
    valid = (j - a >= 0) & (j - a <= 2 * SWA_W) & (kpos >= 0) & (kpos < LAT_T)
    sink_tab = sink_ref[...]
    for c in range(2):
        q = _rope(q_ref[:, c * LANES:(c + 1) * LANES], cos, sin) * SCALE
        outs = []
        for half in range(2):
            sink = _pick_row(sink_tab, 4 * n + 2 * c + half)
            qm = jnp.where(lo, q, 0.0) if half == 0 else jnp.where(lo, 0.0, q)
            qm = qm.astype(BF16)
            s_loc = jnp.where(valid, _qk(qm, kw), NEG)
            s_ctx = _qk(qm, kcb)
            m = jnp.maximum(jnp.maximum(jnp.max(s_loc, axis=-1, keepdims=True),
                                        jnp.max(s_ctx, axis=-1, keepdims=True)), sink)
            e_loc = jnp.exp(s_loc - m)
            e_ctx = jnp.exp(s_ctx - m)
            den = (jnp.sum(e_loc, axis=-1, keepdims=True) + jnp.sum(e_ctx, axis=-1, keepdims=True)
                   + jnp.exp(sink - m))
            outs.append((_pv(e_loc.astype(BF16), vw) + _pv(e_ctx.astype(BF16), vcb)) / den)
        o_ref[:, c * LANES:(c + 1) * LANES] = jnp.where(lo, outs[0], outs[1]).astype(BF16)


def _swa(l, p, cache_swa, cos_tab, sin_tab, sink_tab):
    nq = LAT_T // SWA_BLK
    lat_blk = CTX_ROWS // LAT_T
    cache_spec = lambda which: pl.BlockSpec(
        (None, None, None, CTX_S, LANES), lambda b, n, i: (b, l, which, 0, 0))
    full = pl.BlockSpec((LAT_T, LANES), lambda b, n, i: (0, 0))
    return pl.pallas_call(
        _swa_kernel,
        grid=(N_LAT_B, 2, nq),
        in_specs=[
            pl.BlockSpec((SWA_BLK, 2 * LANES),
                         lambda b, n, i: (CTX_ROWS // SWA_BLK + b * nq + i, C_SQ // (2 * LANES) + n)),
            pl.BlockSpec((LAT_T, LANES), lambda b, n, i: (lat_blk + b, C_SK // LANES)),
            pl.BlockSpec((LAT_T, LANES), lambda b, n, i: (lat_blk + b, C_SV // LANES)),
            cache_spec(0),
            cache_spec(1),
            full,
            full,
            pl.BlockSpec((None, 8, LANES), lambda b, n, i: (l, 0, 0)),
        ],
        out_specs=pl.BlockSpec((SWA_BLK, 2 * LANES), lambda b, n, i: (b * nq + i, n)),
        out_shape=jax.ShapeDtypeStruct((LAT_ROWS, GW), BF16),
        scratch_shapes=[pltpu.VMEM((SWA_PAD, LANES), BF16), pltpu.VMEM((SWA_PAD, LANES), BF16),
                        pltpu.VMEM((CTX_S, LANES), BF16), pltpu.VMEM((CTX_S, LANES), BF16)],
        compiler_params=_cp(("parallel", "parallel", "arbitrary")),
        name="swa_latent",
    )(p, p, p, cache_swa, cache_swa, cos_tab, sin_tab, sink_tab)


DIFF_TQ = 256
DIFF_KEYS = LAT_T + CTX_S


def _diff_kernel(q_ref, k_ref, v_ref, kc_ref, vc_ref, cos_ref, sin_ref, lam_ref, dg_ref, o_ref,
                 kb_ref, vb_ref, *, lam_init):
    i = pl.program_id(2)

    @pl.when(i == 0)
    def _():
        kb_ref[0:LAT_T, :] = _rope(k_ref[...], cos_ref[...], sin_ref[...]).astype(BF16)
        kb_ref[LAT_T:DIFF_KEYS, :] = kc_ref[...].astype(BF16)
        vb_ref[0:LAT_T, :] = v_ref[...].astype(BF16)
        vb_ref[LAT_T:DIFF_KEYS, :] = vc_ref[...].astype(BF16)

    start = pl.multiple_of(i * DIFF_TQ, DIFF_TQ)
    cos = cos_ref[pl.ds(start, DIFF_TQ), :]
    sin = sin_ref[pl.ds(start, DIFF_TQ), :]
    q = _rope(q_ref[...], cos, sin) * SCALE
    lam = _lambda(lam_ref, lam_init)
    o = _diff_head(q, kb_ref[...], vb_ref[...], lam, dg_ref[...], lam_init, _lane_lo(DIFF_TQ))
    o_ref[...] = o.astype(BF16)


def _diff_latent(l, p, cache_diff, cos_tab, sin_tab, diff_lambda, diff_g, lam_init):
    nq = LAT_T // DIFF_TQ
    lat_blk = CTX_ROWS // LAT_T
    cache_spec = lambda which: pl.BlockSpec(
        (None, None, None, CTX_S, LANES), lambda b, h, i: (b, l, which, 0, h))
    full = pl.BlockSpec((LAT_T, LANES), lambda b, h, i: (0, 0))
    return pl.pallas_call(
        functools.partial(_diff_kernel, lam_init=lam_init),
        grid=(N_LAT_B, GW // LANES, nq),
        in_specs=[
            pl.BlockSpec((DIFF_TQ, LANES),
                         lambda b, h, i: (CTX_ROWS // DIFF_TQ + b * nq + i, C_DQ // LANES + h)),
            pl.BlockSpec((LAT_T, LANES), lambda b, h, i: (lat_blk + b, C_DK // LANES + h)),
            pl.BlockSpec((LAT_T, LANES), lambda b, h, i: (lat_blk + b, C_DV // LANES + h)),
            cache_spec(0),
            cache_spec(1),
            full,
            full,
            pl.BlockSpec((None, 4, HD), lambda b, h, i: (l, 0, 0)),
            pl.BlockSpec((None, 1, LANES), lambda b, h, i: (l, 0, 0)),
        ],
        out_specs=pl.BlockSpec((DIFF_TQ, LANES), lambda b, h, i: (b * nq + i, h)),
        out_shape=jax.ShapeDtypeStruct((LAT_ROWS, GW), BF16),
        scratch_shapes=[pltpu.VMEM((DIFF_KEYS, LANES), BF16), pltpu.VMEM((DIFF_KEYS, LANES), BF16)],
        compiler_params=_cp(("parallel", "parallel", "arbitrary")),
        name="diff_latent",
    )(p, p, p, cache_diff, cache_diff, cos_tab, sin_tab, diff_lambda, diff_g)


def _outproj_kernel(mix_ref, x_ref, g_ref, m_ref, w_ref, x1_ref, h2_ref):
    y = jnp.dot(mix_ref[...], w_ref[...], preferred_element_type=F32)
    x1 = x_ref[...] + m_ref[2:3, :] * (_rms(y) * g_ref[1:2, :])
    x1_ref[...] = x1
    h2 = (_rms(x1) * g_ref[2:3, :]) * (1.0 + m_ref[4:5, :]) + m_ref[3:4, :]
    h2_ref[...] = h2.astype(BF16)


def _outproj(l, mix, x, norm_g, mods, w_out):
    tm = 512
    return pl.pallas_call(
        _outproj_kernel,
        grid=(ROWS // tm,),
        in_specs=[
            pl.BlockSpec((tm, D), lambda i: (i, 0)),
            pl.BlockSpec((tm, D), lambda i: (i, 0)),
            pl.BlockSpec((None, 4, D), lambda i: (l, 0, 0)),
            pl.BlockSpec((None, None, N_MOD, D), lambda i: (l, _row_group(i, tm), 0, 0)),
            pl.BlockSpec((None, D, D), lambda i: (l, 0, 0)),
        ],
        out_specs=[pl.BlockSpec((tm, D), lambda i: (i, 0)),
                   pl.BlockSpec((tm, D), lambda i: (i, 0))],
        out_shape=[jax.ShapeDtypeStruct((ROWS, D), F32), jax.ShapeDtypeStruct((ROWS, D), BF16)],
        compiler_params=_cp(("parallel",)),
        name="outproj",
    )(mix, x, norm_g, mods, w_out)


def _mlp_kernel(h_ref, x1_ref, g_ref, m_ref, w1_ref, w2_ref, o_ref, acc_ref):
    j = pl.program_id(1)
    a = jnp.dot(h_ref[...], w1_ref[...], preferred_element_type=F32)
    a = jnp.square(jnp.maximum(a, 0.0)).astype(BF16)
    part = jnp.dot(a, w2_ref[...], preferred_element_type=F32)

    @pl.when(j == 0)
    def _():
        acc_ref[...] = part

    @pl.when(j > 0)
    def _():
        acc_ref[...] += part

    @pl.when(j == pl.num_programs(1) - 1)
    def _():
        o_ref[...] = x1_ref[...] + m_ref[5:6, :] * (_rms(acc_ref[...]) * g_ref[3:4, :])


def _mlp(l, h2, x1, norm_g, mods, w1, w2):
    tm, th = 512, 1024
    return pl.pallas_call(
        _mlp_kernel,
        grid=(ROWS // tm, HIDDEN // th),
        in_specs=[
            pl.BlockSpec((tm, D), lambda i, j: (i, 0)),
            pl.BlockSpec((tm, D), lambda i, j: (i, 0)),
            pl.BlockSpec((None, 4, D), lambda i, j: (l, 0, 0)),
            pl.BlockSpec((None, None, N_MOD, D), lambda i, j: (l, _row_group(i, tm), 0, 0)),
            pl.BlockSpec((None, D, th), lambda i, j: (l, 0, j)),
            pl.BlockSpec((None, th, D), lambda i, j: (l, j, 0)),
        ],
        out_specs=pl.BlockSpec((tm, D), lambda i, j: (i, 0)),
        out_shape=jax.ShapeDtypeStruct((ROWS, D), F32),
        scratch_shapes=[pltpu.VMEM((tm, D), F32)],
        compiler_params=_cp(("parallel", "arbitrary")),
        name="mlp",
    )(h2, x1, norm_g, mods, w1, w2)


def _rope_tables():
    t = np.arange(LAT_T)
    rows = (t // GRID_W).astype(np.float32)
    cols = (t % GRID_W).astype(np.float32)
    n = HD // 4
    inv = jnp.asarray(10000.0, F32) ** (-jnp.arange(n, dtype=F32) / n)
    ar = jnp.asarray(rows)[:, None] * inv[None]
    ac = jnp.asarray(cols)[:, None] * inv[None]
    cos = jnp.concatenate([jnp.cos(ar), jnp.cos(ar), jnp.cos(ac), jnp.cos(ac)], axis=-1)
    sin = jnp.concatenate([-jnp.sin(ar), jnp.sin(ar), -jnp.sin(ac), jnp.sin(ac)], axis=-1)
    return jnp.tile(cos, (1, 2)), jnp.tile(sin, (1, 2))


def kernel(x_prompt, x_sample, cache_na_kv, cache_swa_kv, cache_diff_kv, c, c_ctx, ada_w, ada_b,
           norm_g, w_in, conv_w, na_rpb, swa_sink, diff_lambda, diff_norm_g, w_out, mlp_w1, mlp_w2):
    cvec = jnp.concatenate([c_ctx[None], c, jnp.zeros((8 - 1 - N_LAT_B, D), F32)], axis=0)
    mods = _modulation(cvec, ada_w, ada_b).reshape(DEPTH, 8, N_MOD, D)

    w_in_b = w_in.astype(BF16)
    w_out_b = w_out.astype(BF16)
    w1_b = mlp_w1.astype(BF16)
    w2_b = mlp_w2.astype(BF16)

    cache_na = cache_na_kv.reshape(N_LAT_B, DEPTH, 2, CTX_S, GW)
    cache_swa = cache_swa_kv.reshape(N_LAT_B, DEPTH, 2, CTX_S, LANES)
    cache_diff = cache_diff_kv.reshape(N_LAT_B, DEPTH, 2, CTX_S, GW)
    cos_tab, sin_tab = _rope_tables()
    sink_tab = jnp.broadcast_to(swa_sink[:, :, None], (DEPTH, 8, LANES))
    diff_g = diff_norm_g.reshape(DEPTH, 1, LANES)

    x = jnp.concatenate([x_prompt.reshape(CTX_ROWS, D), x_sample.reshape(LAT_ROWS, D)], axis=0)
    na_st, swa_st, diff_st = [], [], []
    for l in range(DEPTH):
        lam_init = 0.8 - 0.6 * math.exp(-0.3 * l)
        p = _inproj(l, x, norm_g, mods, w_in_b)

        pc = p[:CTX_ROWS].reshape(N_CTX_B, CTX_S, IN_COLS)
        na_st.append(pc[:, :, C_NAK:C_U].reshape(N_CTX_B, CTX_S, 2, 8, HD).transpose(0, 2, 1, 3, 4))
        swa_st.append(pc[:, :, C_SK:C_DQ].reshape(N_CTX_B, CTX_S, 2, 2, HD).transpose(0, 2, 1, 3, 4))
        diff_st.append(pc[:, :, C_DK:].reshape(N_CTX_B, CTX_S, 2, 4, 2 * HD).transpose(0, 2, 1, 3, 4))

        mix_ctx = _ctx_mixer(l, p, conv_w, sink_tab, diff_lambda, diff_g, lam_init)
        o_a = _na(l, p, cache_na, _na_bias_table(na_rpb[l]))
        o_b = _conv_latent(l, p, conv_w)
        o_c = _swa(l, p, cache_swa, cos_tab, sin_tab, sink_tab)
        o_d = _diff_latent(l, p, cache_diff, cos_tab, sin_tab, diff_lambda, diff_g, lam_init)
        mix = jnp.concatenate([mix_ctx, jnp.concatenate([o_a, o_b, o_c, o_d], axis=1)], axis=0)

        x1, h2 = _outproj(l, mix, x, norm_g, mods, w_out_b)
        x = _mlp(l, h2, x1, norm_g, mods, w1_b, w2_b)

    return (x[:CTX_ROWS].reshape(N_CTX_B, CTX_S, D),
            x[CTX_ROWS:].reshape(N_LAT_B, LAT_T, D),
            jnp.stack(na_st, axis=1), jnp.stack(swa_st, axis=1), jnp.stack(diff_st, axis=1))
```

```python
import functools
import math

import numpy as np
import jax
import jax.numpy as jnp
from jax import lax
from jax.experimental import pallas as pl
from jax.experimental.pallas import tpu as pltpu

F32 = jnp.float32
BF16 = jnp.bfloat16

D = 2048
N_CTX_B, CTX_S = 16, 256
N_LAT_B, LAT_T = 2, 2048
CTX_ROWS = N_CTX_B * CTX_S
LAT_ROWS = N_LAT_B * LAT_T
ROWS = CTX_ROWS + LAT_ROWS
DEPTH = 4
GRID_W = 64
GRID_H = LAT_T // GRID_W
HD = 64
GW = 512
IN_COLS = 5376
HIDDEN = 4 * D
N_MOD = 6
EPS = 1e-6
NEG = -1e30
SCALE = HD ** -0.5
NA_R, NA_C = 8, 16
SWA_W = 128
LANES = 128

C_NAQ, C_NAK, C_NAV = 0, 512, 1024
C_U, C_GB, C_GC = 1536, 2048, 2560
C_SQ, C_SK, C_SV = 3072, 3584, 3712
C_DQ, C_DK, C_DV = 3840, 4352, 4864

VMEM_LIMIT = 56 * 1024 * 1024


def _cp(sem):
    return pltpu.CompilerParams(dimension_semantics=sem, vmem_limit_bytes=VMEM_LIMIT)


def _rms(x):
    return x * lax.rsqrt(jnp.mean(x * x, axis=-1, keepdims=True) + EPS)


def _qk(q, k):
    return lax.dot_general(q, k, (((1,), (1,)), ((), ())), preferred_element_type=F32)


def _pv(p, v):
    return jnp.dot(p, v, preferred_element_type=F32)


def _lane_lo(rows):
    return lax.broadcasted_iota(jnp.int32, (rows, LANES), 1) < HD


def _row_group(i, tm):
    return jnp.maximum(i * tm - LAT_T, 0) // LAT_T


def _mod_kernel(c_ref, w_ref, b_ref, o_ref):
    cv = c_ref[...]
    s = cv * (1.0 / (1.0 + jnp.exp(-cv)))
    o_ref[...] = jnp.dot(s.astype(BF16), w_ref[...].astype(BF16),
                         preferred_element_type=F32) + b_ref[...]


def _modulation(cvec, ada_w, ada_b):
    tn = 1024
    n = N_MOD * D
    return pl.pallas_call(
        _mod_kernel,
        grid=(DEPTH, n // tn),
        in_specs=[
            pl.BlockSpec((8, D), lambda l, j: (0, 0)),
            pl.BlockSpec((None, D, tn), lambda l, j: (l, 0, j)),
            pl.BlockSpec((None, 1, tn), lambda l, j: (l, 0, j)),
        ],
        out_specs=pl.BlockSpec((None, 8, tn), lambda l, j: (l, 0, j)),
        out_shape=jax.ShapeDtypeStruct((DEPTH, 8, n), F32),
        compiler_params=_cp(("parallel", "parallel")),
        name="modulation",
    )(cvec, ada_w, ada_b.reshape(DEPTH, 1, n))


def _inproj_kernel(x_ref, g_ref, m_ref, w_ref, o_ref, h_ref):
    @pl.when(pl.program_id(1) == 0)
    def _():
        h = _rms(x_ref[...]) * g_ref[0:1, :]
        h = h * (1.0 + m_ref[1:2, :]) + m_ref[0:1, :]
        h_ref[...] = h.astype(BF16)

    o_ref[...] = jnp.dot(h_ref[...], w_ref[...].astype(BF16), preferred_element_type=F32)


def _inproj(l, x, norm_g, mods, w_in):
    tm, tn = 1024, 768
    return pl.pallas_call(
        _inproj_kernel,
        grid=(ROWS // tm, IN_COLS // tn),
        in_specs=[
            pl.BlockSpec((tm, D), lambda i, j: (i, 0)),
            pl.BlockSpec((None, 4, D), lambda i, j: (l, 0, 0)),
            pl.BlockSpec((None, None, N_MOD, D), lambda i, j: (l, _row_group(i, tm), 0, 0)),
            pl.BlockSpec((None, D, tn), lambda i, j: (l, 0, j)),
        ],
        out_specs=pl.BlockSpec((tm, tn), lambda i, j: (i, j)),
        out_shape=jax.ShapeDtypeStruct((ROWS, IN_COLS), F32),
        scratch_shapes=[pltpu.VMEM((tm, D), BF16)],
        compiler_params=_cp(("parallel", "arbitrary")),
        name="inproj",
    )(x, norm_g, mods, w_in)


def _conv3(u, gb, gc, w):
    z = gc * u
    n = z.shape[0]
    row = lax.broadcasted_iota(jnp.int32, z.shape, 0)
    zp = jnp.where(row == 0, 0.0, pltpu.roll(z, 1, 0))
    zn = jnp.where(row == n - 1, 0.0, pltpu.roll(z, n - 1, 0))
    return gb * (zp * w[0:1, :] + z * w[1:2, :] + zn * w[2:3, :])


def _lambda(lam_ref, lam_init):
    lp = lam_ref[...]
    a = jnp.sum(lp[0:1, :] * lp[1:2, :], axis=-1, keepdims=True)
    b = jnp.sum(lp[2:3, :] * lp[3:4, :], axis=-1, keepdims=True)
    return jnp.exp(a) - jnp.exp(b) + lam_init


def _softmax(s):
    e = jnp.exp(s - jnp.max(s, axis=-1, keepdims=True))
    return e / jnp.sum(e, axis=-1, keepdims=True)


def _diff_head(q, kb, vb, lam, dg, lam_init, lo):
    a1 = _softmax(_qk(jnp.where(lo, q, 0.0).astype(BF16), kb))
    a2 = _softmax(_qk(jnp.where(lo, 0.0, q).astype(BF16), kb))
    o = _pv((a1 - lam * a2).astype(BF16), vb)
    return (_rms(o) * dg) * (1.0 - lam_init)


def _double_head(x, own):
    return jnp.where(own, x, pltpu.roll(x, HD, 1))


def _pick_row(tab, idx):
    row = lax.broadcasted_iota(jnp.int32, tab.shape, 0)
    return jnp.sum(jnp.where(row == idx, tab, 0.0), axis=0, keepdims=True)[:, 0:1]


def _swap16(x):
    lane = lax.broadcasted_iota(jnp.int32, x.shape, 1)
    return jnp.where(lane % 32 < 16, pltpu.roll(x, LANES - 16, 1), pltpu.roll(x, 16, 1))


def _rope(x, cos, sin):
    return x * cos + _swap16(x) * sin


def _ctx_kernel(p_ref, cw_ref, sink_ref, lam_ref, dg_ref, *rest, lam_init):
    o_ref, na_ref, swa_ref, df_ref = rest[-4:]
    s_len = CTX_S
    lo = _lane_lo(s_len)
    lane_hi = jnp.logical_not(lo)

    def blk(c):
        return p_ref[:, c:c + LANES]

    na_ref[0] = p_ref[:, C_NAK:C_NAK + GW]
    na_ref[1] = p_ref[:, C_NAV:C_NAV + GW]
    swa_ref[0] = p_ref[:, C_SK:C_SK + LANES]
    swa_ref[1] = p_ref[:, C_SV:C_SV + LANES]
    df_ref[0] = p_ref[:, C_DK:C_DK + GW]
    df_ref[1] = p_ref[:, C_DV:C_DV + GW]

    for hp in range(GW // LANES):
        q = blk(C_NAQ + hp * LANES) * SCALE
        kb = blk(C_NAK + hp * LANES).astype(BF16)
        vb = blk(C_NAV + hp * LANES).astype(BF16)
        outs = []
        for half in range(2):
            qm = jnp.where(lo if half == 0 else lane_hi, q, 0.0).astype(BF16)
            s = _qk(qm, kb)
            e = jnp.exp(s - jnp.max(s, axis=-1, keepdims=True))
            outs.append(_pv(e.astype(BF16), vb) / jnp.sum(e, axis=-1, keepdims=True))
        o_ref[:, hp * LANES:(hp + 1) * LANES] = jnp.where(lo, outs[0], outs[1]).astype(BF16)

    for cb in range(GW // LANES):
        c = cb * LANES
        y = _conv3(blk(C_U + c), blk(C_GB + c), blk(C_GC + c), cw_ref[:, c:c + LANES])
        o_ref[:, GW + c:GW + c + LANES] = y.astype(BF16)

    k2 = blk(C_SK)
    v2 = blk(C_SV)
    sink_tab = sink_ref[...]
    for n in range(2):
        own = lo if n == 0 else lane_hi
        kd = _double_head(k2, own).astype(BF16)
        vd = _double_head(v2, own).astype(BF16)
        for c in range(2):
            q = blk(C_SQ + (2 * n + c) * LANES) * SCALE
            outs = []
            for half in range(2):
                h = 4 * n + 2 * c + half
                sink = _pick_row(sink_tab, h)
                qm = jnp.where(lo if half == 0 else lane_hi, q, 0.0).astype(BF16)
                s = _qk(qm, kd)
                m = jnp.maximum(jnp.max(s, axis=-1, keepdims=True), sink)
                e = jnp.exp(s - m)
                den = jnp.sum(e, axis=-1, keepdims=True) + jnp.exp(sink - m)
                outs.append(_pv(e.astype(BF16), vd) / den)
            col = 2 * GW + (2 * n + c) * LANES
            o_ref[:, col:col + LANES] = jnp.where(lo, outs[0], outs[1]).astype(BF16)

    lam = _lambda(lam_ref, lam_init)
    dg = dg_ref[...]
    for h in range(GW // LANES):
        q = blk(C_DQ + h * LANES) * SCALE
        kb = blk(C_DK + h * LANES).astype(BF16)
        vb = blk(C_DV + h * LANES).astype(BF16)
        o = _diff_head(q, kb, vb, lam, dg, lam_init, lo)
        col = 3 * GW + h * LANES
        o_ref[:, col:col + LANES] = o.astype(BF16)


def _ctx_mixer(l, p, conv_w, sink_tab, diff_lambda, diff_g, lam_init, kv_prev):
    kv_shapes = [(N_CTX_B, DEPTH, 2, CTX_S, w) for w in (GW, LANES, GW)]
    kv_spec = lambda w: pl.BlockSpec((None, None, 2, CTX_S, w), lambda b: (b, l, 0, 0, 0))
    n_in = 5
    aliases = {n_in + k: 1 + k for k in range(len(kv_prev))}
    return pl.pallas_call(
        functools.partial(_ctx_kernel, lam_init=lam_init),
        grid=(N_CTX_B,),
        in_specs=[
            pl.BlockSpec((CTX_S, IN_COLS), lambda b: (b, 0)),
            pl.BlockSpec((None, 3, GW), lambda b: (l, 0, 0)),
            pl.BlockSpec((None, 8, LANES), lambda b: (l, 0, 0)),
            pl.BlockSpec((None, 4, HD), lambda b: (l, 0, 0)),
            pl.BlockSpec((None, 1, LANES), lambda b: (l, 0, 0)),
        ] + [pl.BlockSpec(memory_space=pl.ANY)] * len(kv_prev),
        out_specs=[pl.BlockSpec((CTX_S, D), lambda b: (b, 0)), kv_spec(GW), kv_spec(LANES), kv_spec(GW)],
        out_shape=[jax.ShapeDtypeStruct((CTX_ROWS, D), BF16)]
        + [jax.ShapeDtypeStruct(s, F32) for s in kv_shapes],
        input_output_aliases=aliases,
        compiler_params=_cp(("parallel",)),
        name="ctx_mixer",
    )(p, conv_w, sink_tab, diff_lambda, diff_g, *kv_prev)


def _na_build_bias(rpb_ref, bt_ref):
    n_dr = 2 * NA_R - 1
    for hh in range(2):
        for dr in range(n_dr):
            row = jnp.broadcast_to(rpb_ref[hh, dr:dr + 1, :], (GRID_W, LANES))
            t_lo = pltpu.roll(row, LANES - (NA_C - 1), 1, stride=1, stride_axis=0)
            t_hi = pltpu.roll(row, HD - (NA_C - 1), 1, stride=1, stride_axis=0)
            for j in range(NA_R):
                d0 = dr - j
                if 0 <= d0 < NA_R:
                    c = (j // 2) * LANES
                    if j % 2 == 0:
                        bt_ref[hh, d0, :, c:c + HD] = t_lo[:, 0:HD]
                    else:
                        bt_ref[hh, d0, :, c + HD:c + LANES] = t_hi[:, HD:LANES]


def _na_kernel(q_ref, k_ref, v_ref, kc_ref, vc_ref, rpb_ref, o_ref, kb_ref, vb_ref, bt_ref):
    _na_build_bias(rpb_ref, bt_ref)
    kb_ref[...] = k_ref[...].astype(BF16)
    vb_ref[...] = v_ref[...].astype(BF16)
    kcb = kc_ref[...].astype(BF16)
    vcb = vc_ref[...].astype(BF16)
    lo = _lane_lo(GRID_W)
    nk = NA_R * GRID_W
    qc = lax.broadcasted_iota(jnp.int32, (GRID_W, nk), 0)
    kc = lax.broadcasted_iota(jnp.int32, (GRID_W, nk), 1) % GRID_W
    cs = jnp.clip(qc - NA_C // 2, 0, GRID_W - NA_C)
    col_ok = (kc >= cs) & (kc < cs + NA_C)

    def body(r, carry):
        rs = jnp.clip(r - NA_R // 2, 0, GRID_H - NA_R)
        d0 = rs - r + NA_R - 1
        q = q_ref[pl.ds(pl.multiple_of(r * GRID_W, GRID_W), GRID_W), :] * SCALE
        kw = kb_ref[pl.ds(pl.multiple_of(rs * GRID_W, GRID_W), nk), :]
        vw = vb_ref[pl.ds(pl.multiple_of(rs * GRID_W, GRID_W), nk), :]
        outs = []
        for half in range(2):
            qm = jnp.where(lo, q, 0.0) if half == 0 else jnp.where(lo, 0.0, q)
            qm = qm.astype(BF16)
            s_loc = jnp.where(col_ok, _qk(qm, kw) + bt_ref[half, d0], NEG)
            s_ctx = _qk(qm, kcb)
            m = jnp.maximum(jnp.max(s_loc, axis=-1, keepdims=True),
                            jnp.max(s_ctx, axis=-1, keepdims=True))
            e_loc = jnp.exp(s_loc - m)
            e_ctx = jnp.exp(s_ctx - m)
            den = jnp.sum(e_loc, axis=-1, keepdims=True) + jnp.sum(e_ctx, axis=-1, keepdims=True)
            outs.append((_pv(e_loc.astype(BF16), vw) + _pv(e_ctx.astype(BF16), vcb)) / den)
        o_ref[pl.ds(pl.multiple_of(r * GRID_W, GRID_W), GRID_W), :] = (
            jnp.where(lo, outs[0], outs[1]).astype(BF16))
        return carry

    lax.fori_loop(0, GRID_H, body, 0)


def _na(l, p, cache_na, rpb_pad):
    lat_blk = CTX_ROWS // LAT_T
    cache_spec = lambda which: pl.BlockSpec(
        (None, None, None, CTX_S, LANES), lambda b, hp: (b, l, which, 0, hp))
    return pl.pallas_call(
        _na_kernel,
        grid=(N_LAT_B, GW // LANES),
        in_specs=[
            pl.BlockSpec((LAT_T, LANES), lambda b, hp: (lat_blk + b, C_NAQ // LANES + hp)),
            pl.BlockSpec((LAT_T, LANES), lambda b, hp: (lat_blk + b, C_NAK // LANES + hp)),
            pl.BlockSpec((LAT_T, LANES), lambda b, hp: (lat_blk + b, C_NAV // LANES + hp)),
            cache_spec(0),
            cache_spec(1),
            pl.BlockSpec((None, 2, 2 * NA_R - 1, LANES), lambda b, hp: (l, hp, 0, 0)),
        ],
        out_specs=pl.BlockSpec((LAT_T, LANES), lambda b, hp: (b, hp)),
        out_shape=jax.ShapeDtypeStruct((LAT_ROWS, GW), BF16),
        scratch_shapes=[pltpu.VMEM((LAT_T, LANES), BF16), pltpu.VMEM((LAT_T, LANES), BF16),
                        pltpu.VMEM((2, NA_R, GRID_W, NA_R * GRID_W), F32)],
        compiler_params=_cp(("parallel", "parallel")),
        name="na_latent",
    )(p, p, p, cache_na, cache_na, rpb_pad)


def _conv_kernel(u_ref, gb_ref, gc_ref, w_ref, o_ref):
    o_ref[...] = _conv3(u_ref[...], gb_ref[...], gc_ref[...], w_ref[...]).astype(BF16)


def _conv_latent(l, p, conv_w):
    lat_blk = CTX_ROWS // LAT_T
    spec = lambda c0: pl.BlockSpec((LAT_T, LANES), lambda b, c: (lat_blk + b, c0 // LANES + c))
    return pl.pallas_call(
        _conv_kernel,
        grid=(N_LAT_B, GW // LANES),
        in_specs=[spec(C_U), spec(C_GB), spec(C_GC),
                  pl.BlockSpec((None, 3, LANES), lambda b, c: (l, 0, c))],
        out_specs=pl.BlockSpec((LAT_T, LANES), lambda b, c: (b, c)),
        out_shape=jax.ShapeDtypeStruct((LAT_ROWS, GW), BF16),
        compiler_params=_cp(("parallel", "parallel")),
        name="conv_latent",
    )(p, p, p, conv_w)


SWA_BLK = 128
SWA_KEYS = SWA_BLK + 2 * SWA_W
SWA_PAD = LAT_T + 2 * SWA_W


def _swa_kernel(q_ref, k_ref, v_ref, kc_ref, vc_ref, cos_ref, sin_ref, sink_ref, o_ref,
                kd_ref, vd_ref, kcd_ref, vcd_ref):
    n = pl.program_id(1)
    i = pl.program_id(2)

    @pl.when(i == 0)
    def _():
        own = jnp.logical_not(_lane_lo(LAT_T)).astype(jnp.int32) == n
        own_c = jnp.logical_not(_lane_lo(CTX_S)).astype(jnp.int32) == n
        kr = _rope(k_ref[...], cos_ref[...], sin_ref[...])
        zeros = jnp.zeros((SWA_W, LANES), BF16)
        kd_ref[0:SWA_W, :] = zeros
        kd_ref[SWA_W + LAT_T:SWA_PAD, :] = zeros
        vd_ref[0:SWA_W, :] = zeros
        vd_ref[SWA_W + LAT_T:SWA_PAD, :] = zeros
        kd_ref[SWA_W:SWA_W + LAT_T, :] = _double_head(kr, own).astype(BF16)
        vd_ref[SWA_W:SWA_W + LAT_T, :] = _double_head(v_ref[...], own).astype(BF16)
        kcd_ref[...] = _double_head(kc_ref[...], own_c).astype(BF16)
        vcd_ref[...] = _double_head(vc_ref[...], own_c).astype(BF16)

    lo = _lane_lo(SWA_BLK)
    start = pl.multiple_of(i * SWA_BLK, SWA_BLK)
    cos = cos_ref[pl.ds(start, SWA_BLK), :]
    sin = sin_ref[pl.ds(start, SWA_BLK), :]
    kw = kd_ref[pl.ds(start, SWA_KEYS), :]
    vw = vd_ref[pl.ds(start, SWA_KEYS), :]
    kcb = kcd_ref[...]
    vcb = vcd_ref[...]
    a = lax.broadcasted_iota(jnp.int32, (SWA_BLK, SWA_KEYS), 0)
    j = lax.broadcasted_iota(jnp.int32, (SWA_BLK, SWA_KEYS), 1)
    kpos = start - SWA_W + j
    valid = (j - a >= 0) & (j - a <= 2 * SWA_W) & (kpos >= 0) & (kpos < LAT_T)
    sink_tab = sink_ref[...]
    for c in range(2):
        q = _rope(q_ref[:, c * LANES:(c + 1) * LANES], cos, sin) * SCALE
        outs = []
        for half in range(2):
            sink = _pick_row(sink_tab, 4 * n + 2 * c + half)
            qm = jnp.where(lo, q, 0.0) if half == 0 else jnp.where(lo, 0.0, q)
            qm = qm.astype(BF16)
            s_loc = jnp.where(valid, _qk(qm, kw), NEG)
            s_ctx = _qk(qm, kcb)
            m = jnp.maximum(jnp.maximum(jnp.max(s_loc, axis=-1, keepdims=True),
                                        jnp.max(s_ctx, axis=-1, keepdims=True)), sink)
            e_loc = jnp.exp(s_loc - m)
            e_ctx = jnp.exp(s_ctx - m)
            den = (jnp.sum(e_loc, axis=-1, keepdims=True) + jnp.sum(e_ctx, axis=-1, keepdims=True)
                   + jnp.exp(sink - m))
            outs.append((_pv(e_loc.astype(BF16), vw) + _pv(e_ctx.astype(BF16), vcb)) / den)
        o_ref[:, c * LANES:(c + 1) * LANES] = jnp.where(lo, outs[0], outs[1]).astype(BF16)


def _swa(l, p, cache_swa, cos_tab, sin_tab, sink_tab):
    nq = LAT_T // SWA_BLK
    lat_blk = CTX_ROWS // LAT_T
    cache_spec = lambda which: pl.BlockSpec(
        (None, None, None, CTX_S, LANES), lambda b, n, i: (b, l, which, 0, 0))
    full = pl.BlockSpec((LAT_T, LANES), lambda b, n, i: (0, 0))
    return pl.pallas_call(
        _swa_kernel,
        grid=(N_LAT_B, 2, nq),
        in_specs=[
            pl.BlockSpec((SWA_BLK, 2 * LANES),
                         lambda b, n, i: (CTX_ROWS // SWA_BLK + b * nq + i, C_SQ // (2 * LANES) + n)),
            pl.BlockSpec((LAT_T, LANES), lambda b, n, i: (lat_blk + b, C_SK // LANES)),
            pl.BlockSpec((LAT_T, LANES), lambda b, n, i: (lat_blk + b, C_SV // LANES)),
            cache_spec(0),
            cache_spec(1),
            full,
            full,
            pl.BlockSpec((None, 8, LANES), lambda b, n, i: (l, 0, 0)),
        ],
        out_specs=pl.BlockSpec((SWA_BLK, 2 * LANES), lambda b, n, i: (b * nq + i, n)),
        out_shape=jax.ShapeDtypeStruct((LAT_ROWS, GW), BF16),
        scratch_shapes=[pltpu.VMEM((SWA_PAD, LANES), BF16), pltpu.VMEM((SWA_PAD, LANES), BF16),
                        pltpu.VMEM((CTX_S, LANES), BF16), pltpu.VMEM((CTX_S, LANES), BF16)],
        compiler_params=_cp(("parallel", "parallel", "arbitrary")),
        name="swa_latent",
    )(p, p, p, cache_swa, cache_swa, cos_tab, sin_tab, sink_tab)


DIFF_TQ = 256
DIFF_KEYS = LAT_T + CTX_S


def _diff_kernel(q_ref, k_ref, v_ref, kc_ref, vc_ref, cos_ref, sin_ref, lam_ref, dg_ref, o_ref,
                 kb_ref, vb_ref, *, lam_init):
    i = pl.program_id(2)

    @pl.when(i == 0)
    def _():
        kb_ref[0:LAT_T, :] = _rope(k_ref[...], cos_ref[...], sin_ref[...]).astype(BF16)
        kb_ref[LAT_T:DIFF_KEYS, :] = kc_ref[...].astype(BF16)
        vb_ref[0:LAT_T, :] = v_ref[...].astype(BF16)
        vb_ref[LAT_T:DIFF_KEYS, :] = vc_ref[...].astype(BF16)

    start = pl.multiple_of(i * DIFF_TQ, DIFF_TQ)
    cos = cos_ref[pl.ds(start, DIFF_TQ), :]
    sin = sin_ref[pl.ds(start, DIFF_TQ), :]
    q = _rope(q_ref[...], cos, sin) * SCALE
    lam = _lambda(lam_ref, lam_init)
    o = _diff_head(q, kb_ref[...], vb_ref[...], lam, dg_ref[...], lam_init, _lane_lo(DIFF_TQ))
    o_ref[...] = o.astype(BF16)


def _diff_latent(l, p, cache_diff, cos_tab, sin_tab, diff_lambda, diff_g, lam_init):
    nq = LAT_T // DIFF_TQ
    lat_blk = CTX_ROWS // LAT_T
    cache_spec = lambda which: pl.BlockSpec(
        (None, None, None, CTX_S, LANES), lambda b, h, i: (b, l, which, 0, h))
    full = pl.BlockSpec((LAT_T, LANES), lambda b, h, i: (0, 0))
    return pl.pallas_call(
        functools.partial(_diff_kernel, lam_init=lam_init),
        grid=(N_LAT_B, GW // LANES, nq),
        in_specs=[
            pl.BlockSpec((DIFF_TQ, LANES),
                         lambda b, h, i: (CTX_ROWS // DIFF_TQ + b * nq + i, C_DQ // LANES + h)),
            pl.BlockSpec((LAT_T, LANES), lambda b, h, i: (lat_blk + b, C_DK // LANES + h)),
            pl.BlockSpec((LAT_T, LANES), lambda b, h, i: (lat_blk + b, C_DV // LANES + h)),
            cache_spec(0),
            cache_spec(1),
            full,
            full,
            pl.BlockSpec((None, 4, HD), lambda b, h, i: (l, 0, 0)),
            pl.BlockSpec((None, 1, LANES), lambda b, h, i: (l, 0, 0)),
        ],
        out_specs=pl.BlockSpec((DIFF_TQ, LANES), lambda b, h, i: (b * nq + i, h)),
        out_shape=jax.ShapeDtypeStruct((LAT_ROWS, GW), BF16),
        scratch_shapes=[pltpu.VMEM((DIFF_KEYS, LANES), BF16), pltpu.VMEM((DIFF_KEYS, LANES), BF16)],
        compiler_params=_cp(("parallel", "parallel", "arbitrary")),
        name="diff_latent",
    )(p, p, p, cache_diff, cache_diff, cos_tab, sin_tab, diff_lambda, diff_g)


def _outproj_kernel(mc_ref, oa_ref, ob_ref, oc_ref, od_ref, x_ref, g_ref, m_ref, w_ref,
                    x1_ref, h2_ref, y_ref, *, n_ctx_blocks):
    i = pl.program_id(0)

    @pl.when(i < n_ctx_blocks)
    def _():
        y_ref[...] = jnp.dot(mc_ref[...], w_ref[...], preferred_element_type=F32)

    @pl.when(i >= n_ctx_blocks)
    def _():
        mix = jnp.concatenate([oa_ref[...], ob_ref[...], oc_ref[...], od_ref[...]], axis=1)
        y_ref[...] = jnp.dot(mix, w_ref[...], preferred_element_type=F32)

    y = y_ref[...]
    x1 = x_ref[...] + m_ref[2:3, :] * (_rms(y) * g_ref[1:2, :])
    x1_ref[...] = x1
    h2 = (_rms(x1) * g_ref[2:3, :]) * (1.0 + m_ref[4:5, :]) + m_ref[3:4, :]
    h2_ref[...] = h2.astype(BF16)


def _outproj(l, mix_ctx, lat_parts, x, norm_g, mods, w_out):
    tm = 512
    nc = CTX_ROWS // tm
    lat_spec = pl.BlockSpec((tm, GW), lambda i: (jnp.maximum(i - nc, 0), 0))
    return pl.pallas_call(
        functools.partial(_outproj_kernel, n_ctx_blocks=nc),
        grid=(ROWS // tm,),
        in_specs=[
            pl.BlockSpec((tm, D), lambda i: (jnp.minimum(i, nc - 1), 0)),
            lat_spec, lat_spec, lat_spec, lat_spec,
            pl.BlockSpec((tm, D), lambda i: (i, 0)),
            pl.BlockSpec((None, 4, D), lambda i: (l, 0, 0)),
            pl.BlockSpec((None, None, N_MOD, D), lambda i: (l, _row_group(i, tm), 0, 0)),
            pl.BlockSpec((None, D, D), lambda i: (l, 0, 0)),
        ],
        out_specs=[pl.BlockSpec((tm, D), lambda i: (i, 0)),
                   pl.BlockSpec((tm, D), lambda i: (i, 0))],
        out_shape=[jax.ShapeDtypeStruct((ROWS, D), F32), jax.ShapeDtypeStruct((ROWS, D), BF16)],
        scratch_shapes=[pltpu.VMEM((tm, D), F32)],
        compiler_params=_cp(("parallel",)),
        name="outproj",
    )(mix_ctx, *lat_parts, x, norm_g, mods, w_out)


def _mlp_kernel(h_ref, x1_ref, g_ref, m_ref, w1_ref, w2_ref, o_ref, acc_ref):
    j = pl.program_id(1)
    a = jnp.dot(h_ref[...], w1_ref[...], preferred_element_type=F32)
    a = jnp.square(jnp.maximum(a, 0.0)).astype(BF16)
    part = jnp.dot(a, w2_ref[...], preferred_element_type=F32)

    @pl.when(j == 0)
    def _():
        acc_ref[...] = part

    @pl.when(j > 0)
    def _():
        acc_ref[...] += part

    @pl.when(j == pl.num_programs(1) - 1)
    def _():
        o_ref[...] = x1_ref[...] + m_ref[5:6, :] * (_rms(acc_ref[...]) * g_ref[3:4, :])


def _mlp(l, h2, x1, norm_g, mods, w1, w2):
    tm, th = 512, 1024
    return pl.pallas_call(
        _mlp_kernel,
        grid=(ROWS // tm, HIDDEN // th),
        in_specs=[
            pl.BlockSpec((tm, D), lambda i, j: (i, 0)),
            pl.BlockSpec((tm, D), lambda i, j: (i, 0)),
            pl.BlockSpec((None, 4, D), lambda i, j: (l, 0, 0)),
            pl.BlockSpec((None, None, N_MOD, D), lambda i, j: (l, _row_group(i, tm), 0, 0)),
            pl.BlockSpec((None, D, th), lambda i, j: (l, 0, j)),
            pl.BlockSpec((None, th, D), lambda i, j: (l, j, 0)),
        ],
        out_specs=pl.BlockSpec((tm, D), lambda i, j: (i, 0)),
        out_shape=jax.ShapeDtypeStruct((ROWS, D), F32),
        scratch_shapes=[pltpu.VMEM((tm, D), F32)],
        compiler_params=_cp(("parallel", "arbitrary")),
        name="mlp",
    )(h2, x1, norm_g, mods, w1, w2)


def _rope_tables():
    t = np.arange(LAT_T)
    rows = (t // GRID_W).astype(np.float32)
    cols = (t % GRID_W).astype(np.float32)
    n = HD // 4
    inv = jnp.asarray(10000.0, F32) ** (-jnp.arange(n, dtype=F32) / n)
    ar = jnp.asarray(rows)[:, None] * inv[None]
    ac = jnp.asarray(cols)[:, None] * inv[None]
    cos = jnp.concatenate([jnp.cos(ar), jnp.cos(ar), jnp.cos(ac), jnp.cos(ac)], axis=-1)
    sin = jnp.concatenate([-jnp.sin(ar), jnp.sin(ar), -jnp.sin(ac), jnp.sin(ac)], axis=-1)
    return jnp.tile(cos, (1, 2)), jnp.tile(sin, (1, 2))


def kernel(x_prompt, x_sample, cache_na_kv, cache_swa_kv, cache_diff_kv, c, c_ctx, ada_w, ada_b,
           norm_g, w_in, conv_w, na_rpb, swa_sink, diff_lambda, diff_norm_g, w_out, mlp_w1, mlp_w2):
    cvec = jnp.concatenate([c_ctx[None], c, jnp.zeros((8 - 1 - N_LAT_B, D), F32)], axis=0)
    mods = _modulation(cvec, ada_w, ada_b).reshape(DEPTH, 8, N_MOD, D)

    w_out_b = w_out.astype(BF16)
    w1_b = mlp_w1.astype(BF16)
    w2_b = mlp_w2.astype(BF16)

    cache_na = cache_na_kv.reshape(N_LAT_B, DEPTH, 2, CTX_S, GW)
    cache_swa = cache_swa_kv.reshape(N_LAT_B, DEPTH, 2, CTX_S, LANES)
    cache_diff = cache_diff_kv.reshape(N_LAT_B, DEPTH, 2, CTX_S, GW)
    cos_tab, sin_tab = _rope_tables()
    sink_tab = jnp.broadcast_to(swa_sink[:, :, None], (DEPTH, 8, LANES))
    diff_g = diff_norm_g.reshape(DEPTH, 1, LANES)

    rpb_pad = jnp.pad(na_rpb, ((0, 0), (0, 0), (0, 0), (0, LANES - na_rpb.shape[-1])))

    x = jnp.concatenate([x_prompt.reshape(CTX_ROWS, D), x_sample.reshape(LAT_ROWS, D)], axis=0)
    kv = []
    for l in range(DEPTH):
        lam_init = 0.8 - 0.6 * math.exp(-0.3 * l)
        p = _inproj(l, x, norm_g, mods, w_in)
        mix_ctx, *kv = _ctx_mixer(l, p, conv_w, sink_tab, diff_lambda, diff_g, lam_init, kv)
        o_a = _na(l, p, cache_na, rpb_pad)
        o_b = _conv_latent(l, p, conv_w)
        o_c = _swa(l, p, cache_swa, cos_tab, sin_tab, sink_tab)
        o_d = _diff_latent(l, p, cache_diff, cos_tab, sin_tab, diff_lambda, diff_g, lam_init)
        x1, h2 = _outproj(l, mix_ctx, (o_a, o_b, o_c, o_d), x, norm_g, mods, w_out_b)
        x = _mlp(l, h2, x1, norm_g, mods, w1_b, w2_b)

    na_kv, swa_kv, diff_kv = kv
    return (x[:CTX_ROWS].reshape(N_CTX_B, CTX_S, D),
            x[CTX_ROWS:].reshape(N_LAT_B, LAT_T, D),
            na_kv.reshape(N_CTX_B, DEPTH, 2, CTX_S, 8, HD),
            swa_kv.reshape(N_CTX_B, DEPTH, 2, CTX_S, 2, HD),
            diff_kv.reshape(N_CTX_B, DEPTH, 2, CTX_S, 4, 2 * HD))
```

```python
import functools
import math

import numpy as np
import jax
import jax.numpy as jnp
from jax import lax
from jax.experimental import pallas as pl
from jax.experimental.pallas import tpu as pltpu

F32 = jnp.float32
BF16 = jnp.bfloat16

D = 2048
N_CTX_B, CTX_S = 16, 256
N_LAT_B, LAT_T = 2, 2048
CTX_ROWS = N_CTX_B * CTX_S
LAT_ROWS = N_LAT_B * LAT_T
ROWS = CTX_ROWS + LAT_ROWS
DEPTH = 4
GRID_W = 64
GRID_H = LAT_T // GRID_W
HD = 64
GW = 512
IN_COLS = 5376
HIDDEN = 4 * D
N_MOD = 6
EPS = 1e-6
NEG = -1e30
SCALE = HD ** -0.5
LOG2E = math.log2(math.e)
Q_SCALE = SCALE * LOG2E
NA_R, NA_C = 8, 16
SWA_W = 128
LANES = 128

C_NAQ, C_NAK, C_NAV = 0, 512, 1024
C_U, C_GB, C_GC = 1536, 2048, 2560
C_SQ, C_SK, C_SV = 3072, 3584, 3712
C_DQ, C_DK, C_DV = 3840, 4352, 4864

VMEM_LIMIT = 56 * 1024 * 1024


def _cp(sem):
    return pltpu.CompilerParams(dimension_semantics=sem, vmem_limit_bytes=VMEM_LIMIT)


def _rms(x):
    return x * lax.rsqrt(jnp.mean(x * x, axis=-1, keepdims=True) + EPS)


def _qk(q, k):
    return lax.dot_general(q, k, (((1,), (1,)), ((), ())), preferred_element_type=F32)


def _pv(p, v):
    return jnp.dot(p, v, preferred_element_type=F32)


def _lane_lo(rows):
    return lax.broadcasted_iota(jnp.int32, (rows, LANES), 1) < HD


def _row_group(i, tm):
    return jnp.maximum(i * tm - LAT_T, 0) // LAT_T


def _mod_kernel(c_ref, w_ref, b_ref, o_ref):
    cv = c_ref[...]
    s = cv * (1.0 / (1.0 + jnp.exp(-cv)))
    o_ref[...] = jnp.dot(s.astype(BF16), w_ref[...].astype(BF16),
                         preferred_element_type=F32) + b_ref[...]


def _modulation(cvec, ada_w, ada_b):
    tn = 1024
    n = N_MOD * D
    return pl.pallas_call(
        _mod_kernel,
        grid=(DEPTH, n // tn),
        in_specs=[
            pl.BlockSpec((8, D), lambda l, j: (0, 0)),
            pl.BlockSpec((None, D, tn), lambda l, j: (l, 0, j)),
            pl.BlockSpec((None, 1, tn), lambda l, j: (l, 0, j)),
        ],
        out_specs=pl.BlockSpec((None, 8, tn), lambda l, j: (l, 0, j)),
        out_shape=jax.ShapeDtypeStruct((DEPTH, 8, n), F32),
        compiler_params=_cp(("parallel", "parallel")),
        name="modulation",
    )(cvec, ada_w, ada_b.reshape(DEPTH, 1, n))


INPROJ_SPLIT = 4


def _inproj_kernel(x_ref, g_ref, m_ref, w_ref, o_ref, h_ref):
    j = pl.program_id(1)
    wb = w_ref[...].astype(BF16)

    @pl.when(j == 0)
    def _():
        sub = x_ref.shape[0] // INPROJ_SPLIT
        for s in range(INPROJ_SPLIT):
            rows = slice(s * sub, (s + 1) * sub)
            h = _rms(x_ref[rows, :]) * g_ref[0:1, :]
            h = (h * (1.0 + m_ref[1:2, :]) + m_ref[0:1, :]).astype(BF16)
            h_ref[rows, :] = h
            o_ref[rows, :] = jnp.dot(h, wb, preferred_element_type=F32)

    @pl.when(j > 0)
    def _():
        o_ref[...] = jnp.dot(h_ref[...], wb, preferred_element_type=F32)


def _inproj(l, x, norm_g, mods, w_in):
    tm, tn = 1024, 768
    return pl.pallas_call(
        _inproj_kernel,
        grid=(ROWS // tm, IN_COLS // tn),
        in_specs=[
            pl.BlockSpec((tm, D), lambda i, j: (i, 0)),
            pl.BlockSpec((None, 4, D), lambda i, j: (l, 0, 0)),
            pl.BlockSpec((None, None, N_MOD, D), lambda i, j: (l, _row_group(i, tm), 0, 0)),
            pl.BlockSpec((None, D, tn), lambda i, j: (l, 0, j)),
        ],
        out_specs=pl.BlockSpec((tm, tn), lambda i, j: (i, j)),
        out_shape=jax.ShapeDtypeStruct((ROWS, IN_COLS), F32),
        scratch_shapes=[pltpu.VMEM((tm, D), BF16)],
        compiler_params=_cp(("parallel", "arbitrary")),
        name="inproj",
    )(x, norm_g, mods, w_in)


def _conv3(u, gb, gc, w):
    z = gc * u
    n = z.shape[0]
    row = lax.broadcasted_iota(jnp.int32, z.shape, 0)
    zp = jnp.where(row == 0, 0.0, pltpu.roll(z, 1, 0))
    zn = jnp.where(row == n - 1, 0.0, pltpu.roll(z, n - 1, 0))
    return gb * (zp * w[0:1, :] + z * w[1:2, :] + zn * w[2:3, :])


def _lambda(lam_ref, lam_init):
    lp = lam_ref[...]
    a = jnp.sum(lp[0:1, :] * lp[1:2, :], axis=-1, keepdims=True)
    b = jnp.sum(lp[2:3, :] * lp[3:4, :], axis=-1, keepdims=True)
    return jnp.exp(a) - jnp.exp(b) + lam_init


def _ones_half(v, lo, half):
    return (jnp.where(lo, v, 1.0) if half == 0 else jnp.where(lo, 1.0, v)).astype(BF16)


def _half_normalise(r, extra=None):
    den = pltpu.roll(r, HD, 1)
    return r / (den if extra is None else den + extra)


def _exp_f32(qm, kb):
    s = _qk(qm.astype(BF16), kb)
    e = jnp.exp2(s - jnp.max(s, axis=-1, keepdims=True))
    return e, jnp.sum(e, axis=-1, keepdims=True)


def _attend(qm, kb, vb):
    e, den = _exp_f32(qm, kb)
    return _pv(e.astype(BF16), vb) / den


def _diff_head(q, kb, vb, lam, dg, lam_init, lo):
    e1, sum1 = _exp_f32(jnp.where(lo, q, 0.0), kb)
    e2, sum2 = _exp_f32(jnp.where(lo, 0.0, q), kb)
    c = lam * sum1 / sum2
    o = _pv((e1 - c * e2).astype(BF16), vb) / sum1
    return (_rms(o) * dg) * (1.0 - lam_init)


def _double_head(x, own):
    return jnp.where(own, x, pltpu.roll(x, HD, 1))


def _pick_row(tab, idx):
    row = lax.broadcasted_iota(jnp.int32, tab.shape, 0)
    return jnp.sum(jnp.where(row == idx, tab, 0.0), axis=0, keepdims=True)[:, 0:1]


def _swap16(x):
    lane = lax.broadcasted_iota(jnp.int32, x.shape, 1)
    return jnp.where(lane % 32 < 16, pltpu.roll(x, LANES - 16, 1), pltpu.roll(x, 16, 1))


def _rope(x, cos, sin):
    return x * cos + _swap16(x) * sin


def _ctx_kernel(p_ref, cw_ref, sink_ref, lam_ref, dg_ref, *rest, lam_init):
    o_ref, na_ref, swa_ref, df_ref = rest[-4:]
    s_len = CTX_S
    lo = _lane_lo(s_len)
    lane_hi = jnp.logical_not(lo)

    def blk(c):
        return p_ref[:, c:c + LANES]

    na_ref[0] = p_ref[:, C_NAK:C_NAK + GW]
    na_ref[1] = p_ref[:, C_NAV:C_NAV + GW]
    swa_ref[0] = p_ref[:, C_SK:C_SK + LANES]
    swa_ref[1] = p_ref[:, C_SV:C_SV + LANES]
    df_ref[0] = p_ref[:, C_DK:C_DK + GW]
    df_ref[1] = p_ref[:, C_DV:C_DV + GW]

    for hp in range(GW // LANES):
        q = blk(C_NAQ + hp * LANES) * Q_SCALE
        kb = blk(C_NAK + hp * LANES).astype(BF16)
        vb = blk(C_NAV + hp * LANES).astype(BF16)
        outs = []
        for half in range(2):
            outs.append(_attend(jnp.where(lo if half == 0 else lane_hi, q, 0.0), kb, vb))
        o_ref[:, hp * LANES:(hp + 1) * LANES] = jnp.where(lo, outs[0], outs[1]).astype(BF16)

    for cb in range(GW // LANES):
        c = cb * LANES
        y = _conv3(blk(C_U + c), blk(C_GB + c), blk(C_GC + c), cw_ref[:, c:c + LANES])
        o_ref[:, GW + c:GW + c + LANES] = y.astype(BF16)

    k2 = blk(C_SK)
    v2 = blk(C_SV)
    sink_tab = sink_ref[...]
    for n in range(2):
        own = lo if n == 0 else lane_hi
        kd = _double_head(k2, own).astype(BF16)
        vd = _double_head(v2, own).astype(BF16)
        for c in range(2):
            q = blk(C_SQ + (2 * n + c) * LANES) * Q_SCALE
            outs = []
            for half in range(2):
                h = 4 * n + 2 * c + half
                sink = _pick_row(sink_tab, h) * LOG2E
                qm = jnp.where(lo if half == 0 else lane_hi, q, 0.0).astype(BF16)
                s = _qk(qm, kd)
                m = jnp.maximum(jnp.max(s, axis=-1, keepdims=True), sink)
                e = jnp.exp2(s - m)
                den = jnp.sum(e, axis=-1, keepdims=True) + jnp.exp2(sink - m)
                outs.append(_pv(e.astype(BF16), vd) / den)
            col = 2 * GW + (2 * n + c) * LANES
            o_ref[:, col:col + LANES] = jnp.where(lo, outs[0], outs[1]).astype(BF16)

    lam = _lambda(lam_ref, lam_init)
    dg = dg_ref[...]
    for h in range(GW // LANES):
        q = blk(C_DQ + h * LANES) * Q_SCALE
        kb = blk(C_DK + h * LANES).astype(BF16)
        o = _diff_head(q, kb, blk(C_DV + h * LANES).astype(BF16), lam, dg, lam_init, lo)
        col = 3 * GW + h * LANES
        o_ref[:, col:col + LANES] = o.astype(BF16)


def _ctx_mixer(l, p, conv_w, sink_tab, diff_lambda, diff_g, lam_init, kv_prev):
    kv_shapes = [(N_CTX_B, DEPTH, 2, CTX_S, w) for w in (GW, LANES, GW)]
    kv_spec = lambda w: pl.BlockSpec((None, None, 2, CTX_S, w), lambda b: (b, l, 0, 0, 0))
    n_in = 5
    aliases = {n_in + k: 1 + k for k in range(len(kv_prev))}
    return pl.pallas_call(
        functools.partial(_ctx_kernel, lam_init=lam_init),
        grid=(N_CTX_B,),
        in_specs=[
            pl.BlockSpec((CTX_S, IN_COLS), lambda b: (b, 0)),
            pl.BlockSpec((None, 3, GW), lambda b: (l, 0, 0)),
            pl.BlockSpec((None, 8, LANES), lambda b: (l, 0, 0)),
            pl.BlockSpec((None, 4, HD), lambda b: (l, 0, 0)),
            pl.BlockSpec((None, 1, LANES), lambda b: (l, 0, 0)),
        ] + [pl.BlockSpec(memory_space=pl.ANY)] * len(kv_prev),
        out_specs=[pl.BlockSpec((CTX_S, D), lambda b: (b, 0)), kv_spec(GW), kv_spec(LANES), kv_spec(GW)],
        out_shape=[jax.ShapeDtypeStruct((CTX_ROWS, D), BF16)]
        + [jax.ShapeDtypeStruct(s, F32) for s in kv_shapes],
        input_output_aliases=aliases,
        compiler_params=_cp(("parallel",)),
        name="ctx_mixer",
    )(p, conv_w, sink_tab, diff_lambda, diff_g, *kv_prev)


def _na_build_bias(rpb_ref, bt_ref):
    n_dr = 2 * NA_R - 1
    for hh in range(2):
        for dr in range(n_dr):
            row = jnp.broadcast_to(rpb_ref[hh, dr:dr + 1, :] * LOG2E, (GRID_W, LANES))
            t_lo = pltpu.roll(row, LANES - (NA_C - 1), 1, stride=1, stride_axis=0)
            t_hi = pltpu.roll(row, HD - (NA_C - 1), 1, stride=1, stride_axis=0)
            for j in range(NA_R):
                d0 = dr - j
                if 0 <= d0 < NA_R:
                    c = (j // 2) * LANES
                    if j % 2 == 0:
                        bt_ref[hh, d0, :, c:c + HD] = t_lo[:, 0:HD]
                    else:
                        bt_ref[hh, d0, :, c + HD:c + LANES] = t_hi[:, HD:LANES]


def _na_kernel(q_ref, k_ref, v_ref, kc_ref, vc_ref, rpb_ref, o_ref, kb_ref, vb_ref, bt_ref):
    _na_build_bias(rpb_ref, bt_ref)
    kb_ref[...] = k_ref[...].astype(BF16)
    kcb = kc_ref[...].astype(BF16)
    lo_t = _lane_lo(LAT_T)
    lo_c = _lane_lo(CTX_S)
    vcb = []
    for half in range(2):
        vb_ref[half] = _ones_half(v_ref[...], lo_t, half)
        vcb.append(_ones_half(vc_ref[...], lo_c, half))
    lo = _lane_lo(GRID_W)
    nk = NA_R * GRID_W
    qc = lax.broadcasted_iota(jnp.int32, (GRID_W, nk), 0)
    kc = lax.broadcasted_iota(jnp.int32, (GRID_W, nk), 1) % GRID_W
    cs = jnp.clip(qc - NA_C // 2, 0, GRID_W - NA_C)
    col_ok = (kc >= cs) & (kc < cs + NA_C)

    def body(r, carry):
        rs = jnp.clip(r - NA_R // 2, 0, GRID_H - NA_R)
        d0 = rs - r + NA_R - 1
        q = q_ref[pl.ds(pl.multiple_of(r * GRID_W, GRID_W), GRID_W), :] * Q_SCALE
        kw = kb_ref[pl.ds(pl.multiple_of(rs * GRID_W, GRID_W), nk), :]
        outs = []
        for half in range(2):
            vw = vb_ref[half, pl.ds(pl.multiple_of(rs * GRID_W, GRID_W), nk), :]
            qm = jnp.where(lo, q, 0.0) if half == 0 else jnp.where(lo, 0.0, q)
            qm = qm.astype(BF16)
            s_loc = jnp.where(col_ok, _qk(qm, kw) + bt_ref[half, d0], NEG)
            s_ctx = _qk(qm, kcb)
            m = jnp.maximum(jnp.max(s_loc, axis=-1, keepdims=True),
                            jnp.max(s_ctx, axis=-1, keepdims=True))
            e_loc = jnp.exp2(s_loc - m).astype(BF16)
            e_ctx = jnp.exp2(s_ctx - m).astype(BF16)
            outs.append(_half_normalise(_pv(e_loc, vw) + _pv(e_ctx, vcb[half])))
        o_ref[pl.ds(pl.multiple_of(r * GRID_W, GRID_W), GRID_W), :] = (
            jnp.where(lo, outs[0], outs[1]).astype(BF16))
        return carry

    lax.fori_loop(0, GRID_H, body, 0, unroll=4)


def _na(l, p, cache_na, rpb_pad):
    lat_blk = CTX_ROWS // LAT_T
    cache_spec = lambda which: pl.BlockSpec(
        (None, None, None, CTX_S, LANES), lambda b, hp: (b, l, which, 0, hp))
    return pl.pallas_call(
        _na_kernel,
        grid=(N_LAT_B, GW // LANES),
        in_specs=[
            pl.BlockSpec((LAT_T, LANES), lambda b, hp: (lat_blk + b, C_NAQ // LANES + hp)),
            pl.BlockSpec((LAT_T, LANES), lambda b, hp: (lat_blk + b, C_NAK // LANES + hp)),
            pl.BlockSpec((LAT_T, LANES), lambda b, hp: (lat_blk + b, C_NAV // LANES + hp)),
            cache_spec(0),
            cache_spec(1),
            pl.BlockSpec((None, 2, 2 * NA_R - 1, LANES), lambda b, hp: (l, hp, 0, 0)),
        ],
        out_specs=pl.BlockSpec((LAT_T, LANES), lambda b, hp: (b, hp)),
        out_shape=jax.ShapeDtypeStruct((LAT_ROWS, GW), BF16),
        scratch_shapes=[pltpu.VMEM((LAT_T, LANES), BF16), pltpu.VMEM((2, LAT_T, LANES), BF16),
                        pltpu.VMEM((2, NA_R, GRID_W, NA_R * GRID_W), F32)],
        compiler_params=_cp(("parallel", "parallel")),
        name="na_latent",
    )(p, p, p, cache_na, cache_na, rpb_pad)


def _conv_kernel(u_ref, gb_ref, gc_ref, w_ref, o_ref):
    o_ref[...] = _conv3(u_ref[...], gb_ref[...], gc_ref[...], w_ref[...]).astype(BF16)


def _conv_latent(l, p, conv_w):
    lat_blk = CTX_ROWS // LAT_T
    spec = lambda c0: pl.BlockSpec((LAT_T, LANES), lambda b, c: (lat_blk + b, c0 // LANES + c))
    return pl.pallas_call(
        _conv_kernel,
        grid=(N_LAT_B, GW // LANES),
        in_specs=[spec(C_U), spec(C_GB), spec(C_GC),
                  pl.BlockSpec((None, 3, LANES), lambda b, c: (l, 0, c))],
        out_specs=pl.BlockSpec((LAT_T, LANES), lambda b, c: (b, c)),
        out_shape=jax.ShapeDtypeStruct((LAT_ROWS, GW), BF16),
        compiler_params=_cp(("parallel", "parallel")),
        name="conv_latent",
    )(p, p, p, conv_w)


SWA_BLK = 128
SWA_KEYS = SWA_BLK + 2 * SWA_W
SWA_PAD = LAT_T + 2 * SWA_W


def _swa_kernel(q_ref, k_ref, v_ref, kc_ref, vc_ref, cos_ref, sin_ref, sink_ref, o_ref,
                kd_ref, vd_ref, kcd_ref, vcd_ref):
    n = pl.program_id(1)
    i = pl.program_id(2)

    @pl.when(i == 0)
    def _():
        own = jnp.logical_not(_lane_lo(LAT_T)).astype(jnp.int32) == n
        own_c = jnp.logical_not(_lane_lo(CTX_S)).astype(jnp.int32) == n
        kr = _rope(k_ref[...], cos_ref[...], sin_ref[...])
        zeros = jnp.zeros((SWA_W, LANES), BF16)
        kd_ref[0:SWA_W, :] = zeros
        kd_ref[SWA_W + LAT_T:SWA_PAD, :] = zeros
        kd_ref[SWA_W:SWA_W + LAT_T, :] = _double_head(kr, own).astype(BF16)
        kcd_ref[...] = _double_head(kc_ref[...], own_c).astype(BF16)
        vd = _double_head(v_ref[...], own)
        vcd = _double_head(vc_ref[...], own_c)
        for half in range(2):
            vd_ref[half, 0:SWA_W, :] = zeros
            vd_ref[half, SWA_W + LAT_T:SWA_PAD, :] = zeros
            vd_ref[half, SWA_W:SWA_W + LAT_T, :] = _ones_half(vd, _lane_lo(LAT_T), half)
            vcd_ref[half] = _ones_half(vcd, _lane_lo(CTX_S), half)

    lo = _lane_lo(SWA_BLK)
    start = pl.multiple_of(i * SWA_BLK, SWA_BLK)
    cos = cos_ref[pl.ds(start, SWA_BLK), :]
    sin = sin_ref[pl.ds(start, SWA_BLK), :]
    kw = kd_ref[pl.ds(start, SWA_KEYS), :]
    kcb = kcd_ref[...]
    a = lax.broadcasted_iota(jnp.int32, (SWA_BLK, SWA_KEYS), 0)
    j = lax.broadcasted_iota(jnp.int32, (SWA_BLK, SWA_KEYS), 1)
    kpos = start - SWA_W + j
    valid = (j - a >= 0) & (j - a <= 2 * SWA_W) & (kpos >= 0) & (kpos < LAT_T)
    sink_tab = sink_ref[...]
    for c in range(2):
        q = _rope(q_ref[:, c * LANES:(c + 1) * LANES], cos, sin) * Q_SCALE
        outs = []
        for half in range(2):
            sink = _pick_row(sink_tab, 4 * n + 2 * c + half) * LOG2E
            qm = jnp.where(lo, q, 0.0) if half == 0 else jnp.where(lo, 0.0, q)
            qm = qm.astype(BF16)
            s_loc = jnp.where(valid, _qk(qm, kw), NEG)
            s_ctx = _qk(qm, kcb)
            m = jnp.maximum(jnp.maximum(jnp.max(s_loc, axis=-1, keepdims=True),
                                        jnp.max(s_ctx, axis=-1, keepdims=True)), sink)
            e_loc = jnp.exp2(s_loc - m).astype(BF16)
            e_ctx = jnp.exp2(s_ctx - m).astype(BF16)
            r = _pv(e_loc, vd_ref[half, pl.ds(start, SWA_KEYS), :]) + _pv(e_ctx, vcd_ref[half])
            outs.append(_half_normalise(r, jnp.exp2(sink - m)))
        o_ref[:, c * LANES:(c + 1) * LANES] = jnp.where(lo, outs[0], outs[1]).astype(BF16)


def _swa(l, p, cache_swa, cos_tab, sin_tab, sink_tab):
    nq = LAT_T // SWA_BLK
    lat_blk = CTX_ROWS // LAT_T
    cache_spec = lambda which: pl.BlockSpec(
        (None, None, None, CTX_S, LANES), lambda b, n, i: (b, l, which, 0, 0))
    full = pl.BlockSpec((LAT_T, LANES), lambda b, n, i: (0, 0))
    return pl.pallas_call(
        _swa_kernel,
        grid=(N_LAT_B, 2, nq),
        in_specs=[
            pl.BlockSpec((SWA_BLK, 2 * LANES),
                         lambda b, n, i: (CTX_ROWS // SWA_BLK + b * nq + i, C_SQ // (2 * LANES) + n)),
            pl.BlockSpec((LAT_T, LANES), lambda b, n, i: (lat_blk + b, C_SK // LANES)),
            pl.BlockSpec((LAT_T, LANES), lambda b, n, i: (lat_blk + b, C_SV // LANES)),
            cache_spec(0),
            cache_spec(1),
            full,
            full,
            pl.BlockSpec((None, 8, LANES), lambda b, n, i: (l, 0, 0)),
        ],
        out_specs=pl.BlockSpec((SWA_BLK, 2 * LANES), lambda b, n, i: (b * nq + i, n)),
        out_shape=jax.ShapeDtypeStruct((LAT_ROWS, GW), BF16),
        scratch_shapes=[pltpu.VMEM((SWA_PAD, LANES), BF16), pltpu.VMEM((2, SWA_PAD, LANES), BF16),
                        pltpu.VMEM((CTX_S, LANES), BF16), pltpu.VMEM((2, CTX_S, LANES), BF16)],
        compiler_params=_cp(("parallel", "parallel", "arbitrary")),
        name="swa_latent",
    )(p, p, p, cache_swa, cache_swa, cos_tab, sin_tab, sink_tab)


DIFF_TQ = 256
DIFF_KEYS = LAT_T + CTX_S


def _diff_kernel(q_ref, k_ref, v_ref, kc_ref, vc_ref, cos_ref, sin_ref, lam_ref, dg_ref, o_ref,
                 kb_ref, vb_ref, *, lam_init):
    i = pl.program_id(2)

    @pl.when(i == 0)
    def _():
        kb_ref[0:LAT_T, :] = _rope(k_ref[...], cos_ref[...], sin_ref[...]).astype(BF16)
        kb_ref[LAT_T:DIFF_KEYS, :] = kc_ref[...].astype(BF16)
        vb_ref[0:LAT_T, :] = v_ref[...].astype(BF16)
        vb_ref[LAT_T:DIFF_KEYS, :] = vc_ref[...].astype(BF16)

    start = pl.multiple_of(i * DIFF_TQ, DIFF_TQ)
    cos = cos_ref[pl.ds(start, DIFF_TQ), :]
    sin = sin_ref[pl.ds(start, DIFF_TQ), :]
    q = _rope(q_ref[...], cos, sin) * Q_SCALE
    lam = _lambda(lam_ref, lam_init)
    o = _diff_head(q, kb_ref[...], vb_ref[...], lam, dg_ref[...], lam_init, _lane_lo(DIFF_TQ))
    o_ref[...] = o.astype(BF16)


def _diff_latent(l, p, cache_diff, cos_tab, sin_tab, diff_lambda, diff_g, lam_init):
    nq = LAT_T // DIFF_TQ
    lat_blk = CTX_ROWS // LAT_T
    cache_spec = lambda which: pl.BlockSpec(
        (None, None, None, CTX_S, LANES), lambda b, h, i: (b, l, which, 0, h))
    full = pl.BlockSpec((LAT_T, LANES), lambda b, h, i: (0, 0))
    return pl.pallas_call(
        functools.partial(_diff_kernel, lam_init=lam_init),
        grid=(N_LAT_B, GW // LANES, nq),
        in_specs=[
            pl.BlockSpec((DIFF_TQ, LANES),
                         lambda b, h, i: (CTX_ROWS // DIFF_TQ + b * nq + i, C_DQ // LANES + h)),
            pl.BlockSpec((LAT_T, LANES), lambda b, h, i: (lat_blk + b, C_DK // LANES + h)),
            pl.BlockSpec((LAT_T, LANES), lambda b, h, i: (lat_blk + b, C_DV // LANES + h)),
            cache_spec(0),
            cache_spec(1),
            full,
            full,
            pl.BlockSpec((None, 4, HD), lambda b, h, i: (l, 0, 0)),
            pl.BlockSpec((None, 1, LANES), lambda b, h, i: (l, 0, 0)),
        ],
        out_specs=pl.BlockSpec((DIFF_TQ, LANES), lambda b, h, i: (b * nq + i, h)),
        out_shape=jax.ShapeDtypeStruct((LAT_ROWS, GW), BF16),
        scratch_shapes=[pltpu.VMEM((DIFF_KEYS, LANES), BF16), pltpu.VMEM((DIFF_KEYS, LANES), BF16)],
        compiler_params=_cp(("parallel", "parallel", "arbitrary")),
        name="diff_latent",
    )(p, p, p, cache_diff, cache_diff, cos_tab, sin_tab, diff_lambda, diff_g)


OUTPROJ_SPLIT = 2


def _outproj_kernel(mc_ref, oa_ref, ob_ref, oc_ref, od_ref, x_ref, g_ref, m_ref, w_ref,
                    x1_ref, h2_ref, *, n_ctx_blocks):
    i = pl.program_id(0)
    tm = x_ref.shape[0]
    sub = tm // OUTPROJ_SPLIT

    def finish(rows, y):
        x1 = x_ref[rows, :] + m_ref[2:3, :] * (_rms(y) * g_ref[1:2, :])
        x1_ref[rows, :] = x1
        h2 = (_rms(x1) * g_ref[2:3, :]) * (1.0 + m_ref[4:5, :]) + m_ref[3:4, :]
        h2_ref[rows, :] = h2.astype(BF16)

    @pl.when(i < n_ctx_blocks)
    def _():
        for s in range(OUTPROJ_SPLIT):
            rows = slice(s * sub, (s + 1) * sub)
            finish(rows, jnp.dot(mc_ref[rows, :], w_ref[...], preferred_element_type=F32))

    @pl.when(i >= n_ctx_blocks)
    def _():
        for s in range(OUTPROJ_SPLIT):
            rows = slice(s * sub, (s + 1) * sub)
            mix = jnp.concatenate([oa_ref[rows, :], ob_ref[rows, :], oc_ref[rows, :], od_ref[rows, :]],
                                  axis=1)
            finish(rows, jnp.dot(mix, w_ref[...], preferred_element_type=F32))


def _outproj(l, mix_ctx, lat_parts, x, norm_g, mods, w_out):
    tm = 512
    nc = CTX_ROWS // tm
    lat_spec = pl.BlockSpec((tm, GW), lambda i: (jnp.maximum(i - nc, 0), 0))
    return pl.pallas_call(
        functools.partial(_outproj_kernel, n_ctx_blocks=nc),
        grid=(ROWS // tm,),
        in_specs=[
            pl.BlockSpec((tm, D), lambda i: (jnp.minimum(i, nc - 1), 0)),
            lat_spec, lat_spec, lat_spec, lat_spec,
            pl.BlockSpec((tm, D), lambda i: (i, 0)),
            pl.BlockSpec((None, 4, D), lambda i: (l, 0, 0)),
            pl.BlockSpec((None, None, N_MOD, D), lambda i: (l, _row_group(i, tm), 0, 0)),
            pl.BlockSpec((None, D, D), lambda i: (l, 0, 0)),
        ],
        out_specs=[pl.BlockSpec((tm, D), lambda i: (i, 0)),
                   pl.BlockSpec((tm, D), lambda i: (i, 0))],
        out_shape=[jax.ShapeDtypeStruct((ROWS, D), F32), jax.ShapeDtypeStruct((ROWS, D), BF16)],
        compiler_params=_cp(("parallel",)),
        name="outproj",
    )(mix_ctx, *lat_parts, x, norm_g, mods, w_out)


def _mlp_kernel(h_ref, x1_ref, g_ref, m_ref, w1_ref, w2_ref, o_ref, acc_ref):
    j = pl.program_id(1)
    a = jnp.dot(h_ref[...], w1_ref[...], preferred_element_type=F32)
    a = jnp.square(jnp.maximum(a, 0.0)).astype(BF16)
    part = jnp.dot(a, w2_ref[...], preferred_element_type=F32)

    @pl.when(j == 0)
    def _():
        acc_ref[...] = part

    @pl.when(j > 0)
    def _():
        acc_ref[...] += part

    @pl.when(j == pl.num_programs(1) - 1)
    def _():
        o_ref[...] = x1_ref[...] + m_ref[5:6, :] * (_rms(acc_ref[...]) * g_ref[3:4, :])


def _mlp(l, h2, x1, norm_g, mods, w1, w2):
    tm, th = 512, 1024
    return pl.pallas_call(
        _mlp_kernel,
        grid=(ROWS // tm, HIDDEN // th),
        in_specs=[
            pl.BlockSpec((tm, D), lambda i, j: (i, 0)),
            pl.BlockSpec((tm, D), lambda i, j: (i, 0)),
            pl.BlockSpec((None, 4, D), lambda i, j: (l, 0, 0)),
            pl.BlockSpec((None, None, N_MOD, D), lambda i, j: (l, _row_group(i, tm), 0, 0)),
            pl.BlockSpec((None, D, th), lambda i, j: (l, 0, j)),
            pl.BlockSpec((None, th, D), lambda i, j: (l, j, 0)),
        ],
        out_specs=pl.BlockSpec((tm, D), lambda i, j: (i, 0)),
        out_shape=jax.ShapeDtypeStruct((ROWS, D), F32),
        scratch_shapes=[pltpu.VMEM((tm, D), F32)],
        compiler_params=_cp(("parallel", "arbitrary")),
        name="mlp",
    )(h2, x1, norm_g, mods, w1, w2)


def _rope_tables():
    t = np.arange(LAT_T)
    rows = (t // GRID_W).astype(np.float32)
    cols = (t % GRID_W).astype(np.float32)
    n = HD // 4
    inv = jnp.asarray(10000.0, F32) ** (-jnp.arange(n, dtype=F32) / n)
    ar = jnp.asarray(rows)[:, None] * inv[None]
    ac = jnp.asarray(cols)[:, None] * inv[None]
    cos = jnp.concatenate([jnp.cos(ar), jnp.cos(ar), jnp.cos(ac), jnp.cos(ac)], axis=-1)
    sin = jnp.concatenate([-jnp.sin(ar), jnp.sin(ar), -jnp.sin(ac), jnp.sin(ac)], axis=-1)
    return jnp.tile(cos, (1, 2)), jnp.tile(sin, (1, 2))


def kernel(x_prompt, x_sample, cache_na_kv, cache_swa_kv, cache_diff_kv, c, c_ctx, ada_w, ada_b,
           norm_g, w_in, conv_w, na_rpb, swa_sink, diff_lambda, diff_norm_g, w_out, mlp_w1, mlp_w2):
    cvec = jnp.concatenate([c_ctx[None], c, jnp.zeros((8 - 1 - N_LAT_B, D), F32)], axis=0)
    mods = _modulation(cvec, ada_w, ada_b).reshape(DEPTH, 8, N_MOD, D)

    w_out_b = w_out.astype(BF16)
    w1_b = mlp_w1.astype(BF16)
    w2_b = mlp_w2.astype(BF16)

    cache_na = cache_na_kv.reshape(N_LAT_B, DEPTH, 2, CTX_S, GW)
    cache_swa = cache_swa_kv.reshape(N_LAT_B, DEPTH, 2, CTX_S, LANES)
    cache_diff = cache_diff_kv.reshape(N_LAT_B, DEPTH, 2, CTX_S, GW)
    cos_tab, sin_tab = _rope_tables()
    sink_tab = jnp.broadcast_to(swa_sink[:, :, None], (DEPTH, 8, LANES))
    diff_g = diff_norm_g.reshape(DEPTH, 1, LANES)

    rpb_pad = jnp.pad(na_rpb, ((0, 0), (0, 0), (0, 0), (0, LANES - na_rpb.shape[-1])))

    x = jnp.concatenate([x_prompt.reshape(CTX_ROWS, D), x_sample.reshape(LAT_ROWS, D)], axis=0)
    kv = []
    for l in range(DEPTH):
        lam_init = 0.8 - 0.6 * math.exp(-0.3 * l)
        p = _inproj(l, x, norm_g, mods, w_in)
        mix_ctx, *kv = _ctx_mixer(l, p, conv_w, sink_tab, diff_lambda, diff_g, lam_init, kv)
        o_a = _na(l, p, cache_na, rpb_pad)
        o_b = _conv_latent(l, p, conv_w)
        o_c = _swa(l, p, cache_swa, cos_tab, sin_tab, sink_tab)
        o_d = _diff_latent(l, p, cache_diff, cos_tab, sin_tab, diff_lambda, diff_g, lam_init)
        x1, h2 = _outproj(l, mix_ctx, (o_a, o_b, o_c, o_d), x, norm_g, mods, w_out_b)
        x = _mlp(l, h2, x1, norm_g, mods, w1_b, w2_b)

    na_kv, swa_kv, diff_kv = kv
    return (x[:CTX_ROWS].reshape(N_CTX_B, CTX_S, D),
            x[CTX_ROWS:].reshape(N_LAT_B, LAT_T, D),
            na_kv.reshape(N_CTX_B, DEPTH, 2, CTX_S, 8, HD),
            swa_kv.reshape(N_CTX_B, DEPTH, 2, CTX_S, 2, HD),
            diff_kv.reshape(N_CTX_B, DEPTH, 2, CTX_S, 4, 2 * HD))
```

```python
import functools
import math

import numpy as np
import jax
import jax.numpy as jnp
from jax import lax
from jax.experimental import pallas as pl
from jax.experimental.pallas import tpu as pltpu

F32 = jnp.float32
BF16 = jnp.bfloat16

D = 2048
N_CTX_B, CTX_S = 16, 256
N_LAT_B, LAT_T = 2, 2048
CTX_ROWS = N_CTX_B * CTX_S
LAT_ROWS = N_LAT_B * LAT_T
ROWS = CTX_ROWS + LAT_ROWS
DEPTH = 4
GRID_W = 64
GRID_H = LAT_T // GRID_W
HD = 64
GW = 512
IN_COLS = 5376
HIDDEN = 4 * D
N_MOD = 6
EPS = 1e-6
NEG = -1e30
SCALE = HD ** -0.5
LOG2E = math.log2(math.e)
Q_SCALE = SCALE * LOG2E
NA_R, NA_C = 8, 16
SWA_W = 128
LANES = 128

C_NAQ, C_NAK, C_NAV = 0, 512, 1024
C_U, C_GB, C_GC = 1536, 2048, 2560
C_SQ, C_SK, C_SV = 3072, 3584, 3712
C_DQ, C_DK, C_DV = 3840, 4352, 4864

VMEM_LIMIT = 56 * 1024 * 1024


def _cp(sem):
    return pltpu.CompilerParams(dimension_semantics=sem, vmem_limit_bytes=VMEM_LIMIT)


def _rms(x):
    return x * lax.rsqrt(jnp.mean(x * x, axis=-1, keepdims=True) + EPS)


def _qk(q, k):
    return lax.dot_general(q, k, (((1,), (1,)), ((), ())), preferred_element_type=F32)


def _pv(p, v):
    return jnp.dot(p, v, preferred_element_type=F32)


def _lane_lo(rows):
    return lax.broadcasted_iota(jnp.int32, (rows, LANES), 1) < HD


def _row_group(i, tm):
    return jnp.maximum(i * tm - LAT_T, 0) // LAT_T


def _mod_kernel(c_ref, w_ref, b_ref, o_ref):
    cv = c_ref[...]
    s = cv * (1.0 / (1.0 + jnp.exp(-cv)))
    o_ref[...] = jnp.dot(s.astype(BF16), w_ref[...].astype(BF16),
                         preferred_element_type=F32) + b_ref[...]


def _modulation(cvec, ada_w, ada_b):
    tn = 1024
    n = N_MOD * D
    return pl.pallas_call(
        _mod_kernel,
        grid=(DEPTH, n // tn),
        in_specs=[
            pl.BlockSpec((8, D), lambda l, j: (0, 0)),
            pl.BlockSpec((None, D, tn), lambda l, j: (l, 0, j)),
            pl.BlockSpec((None, 1, tn), lambda l, j: (l, 0, j)),
        ],
        out_specs=pl.BlockSpec((None, 8, tn), lambda l, j: (l, 0, j)),
        out_shape=jax.ShapeDtypeStruct((DEPTH, 8, n), F32),
        compiler_params=_cp(("parallel", "parallel")),
        name="modulation",
    )(cvec, ada_w, ada_b.reshape(DEPTH, 1, n))


INPROJ_SPLIT = 4


def _inproj_kernel(x_ref, g_ref, m_ref, w_ref, o_ref, h_ref):
    j = pl.program_id(1)
    wb = w_ref[...].astype(BF16)

    @pl.when(j == 0)
    def _():
        sub = x_ref.shape[0] // INPROJ_SPLIT
        for s in range(INPROJ_SPLIT):
            rows = slice(s * sub, (s + 1) * sub)
            h = _rms(x_ref[rows, :]) * g_ref[0:1, :]
            h = (h * (1.0 + m_ref[1:2, :]) + m_ref[0:1, :]).astype(BF16)
            h_ref[rows, :] = h
            o_ref[rows, :] = jnp.dot(h, wb, preferred_element_type=F32)

    @pl.when(j > 0)
    def _():
        o_ref[...] = jnp.dot(h_ref[...], wb, preferred_element_type=F32)


def _inproj(l, x, norm_g, mods, w_in):
    tm, tn = 1024, 768
    return pl.pallas_call(
        _inproj_kernel,
        grid=(ROWS // tm, IN_COLS // tn),
        in_specs=[
            pl.BlockSpec((tm, D), lambda i, j: (i, 0)),
            pl.BlockSpec((None, 4, D), lambda i, j: (l, 0, 0)),
            pl.BlockSpec((None, None, N_MOD, D), lambda i, j: (l, _row_group(i, tm), 0, 0)),
            pl.BlockSpec((None, D, tn), lambda i, j: (l, 0, j)),
        ],
        out_specs=pl.BlockSpec((tm, tn), lambda i, j: (i, j)),
        out_shape=jax.ShapeDtypeStruct((ROWS, IN_COLS), F32),
        scratch_shapes=[pltpu.VMEM((tm, D), BF16)],
        compiler_params=_cp(("parallel", "arbitrary")),
        name="inproj",
    )(x, norm_g, mods, w_in)


def _conv3(u, gb, gc, w):
    z = gc * u
    n = z.shape[0]
    row = lax.broadcasted_iota(jnp.int32, z.shape, 0)
    zp = jnp.where(row == 0, 0.0, pltpu.roll(z, 1, 0))
    zn = jnp.where(row == n - 1, 0.0, pltpu.roll(z, n - 1, 0))
    return gb * (zp * w[0:1, :] + z * w[1:2, :] + zn * w[2:3, :])


def _lambda(lam_ref, lam_init):
    lp = lam_ref[...]
    a = jnp.sum(lp[0:1, :] * lp[1:2, :], axis=-1, keepdims=True)
    b = jnp.sum(lp[2:3, :] * lp[3:4, :], axis=-1, keepdims=True)
    return jnp.exp(a) - jnp.exp(b) + lam_init


def _stack_halves(q, lo):
    return jnp.concatenate([jnp.where(lo, q, 0.0), jnp.where(lo, 0.0, q)], axis=0).astype(BF16)


def _unstack_halves(o, lo):
    n = o.shape[0] // 2
    return jnp.where(lo, o[0:n], o[n:2 * n])


def _with_ones(v):
    return jnp.concatenate([v, jnp.ones_like(v)], axis=1).astype(BF16)


def _row_max(*parts):
    m = jnp.max(parts[0], axis=-1, keepdims=True)
    for p in parts[1:]:
        m = jnp.maximum(m, jnp.max(p, axis=-1, keepdims=True))
    return m


def _wide_div(r, extra=None):
    den = r[:, LANES:2 * LANES]
    return r[:, 0:LANES] / (den if extra is None else den + extra)


def _pair_column(n, first, second):
    row = lax.broadcasted_iota(jnp.int32, (2 * n, 1), 0)
    return jnp.where(row < n, first, second)


def _diff_finish(s1, s2, vb, lam, dg, lam_init):
    e1 = jnp.exp2(s1 - jnp.max(s1, axis=-1, keepdims=True))
    e2 = jnp.exp2(s2 - jnp.max(s2, axis=-1, keepdims=True))
    sum1 = jnp.sum(e1, axis=-1, keepdims=True)
    c = lam * sum1 / jnp.sum(e2, axis=-1, keepdims=True)
    o = _pv((e1 - c * e2).astype(BF16), vb) / sum1
    return (_rms(o) * dg) * (1.0 - lam_init)


def _double_head(x, own):
    return jnp.where(own, x, pltpu.roll(x, HD, 1))


def _pick_row(tab, idx):
    row = lax.broadcasted_iota(jnp.int32, tab.shape, 0)
    return jnp.sum(jnp.where(row == idx, tab, 0.0), axis=0, keepdims=True)[:, 0:1]


def _swap16(x):
    lane = lax.broadcasted_iota(jnp.int32, x.shape, 1)
    return jnp.where(lane % 32 < 16, pltpu.roll(x, LANES - 16, 1), pltpu.roll(x, 16, 1))


def _rope(x, cos, sin):
    return x * cos + _swap16(x) * sin


def _ctx_kernel(p_ref, cw_ref, sink_ref, lam_ref, dg_ref, *rest, lam_init, first_layer):
    o_ref, na_ref, swa_ref, df_ref = rest[-4:]
    n = CTX_S
    lo = _lane_lo(n)
    lane_hi = jnp.logical_not(lo)

    def blk(c):
        return p_ref[:, c:c + LANES]

    kv_parts = ((na_ref, C_NAK, C_NAV, GW), (swa_ref, C_SK, C_SV, LANES), (df_ref, C_DK, C_DV, GW))
    for ref, ck, cv, w in kv_parts:
        if first_layer:
            ref[0, 0] = p_ref[:, ck:ck + w]
            ref[0, 1] = p_ref[:, cv:cv + w]
            ref[1:DEPTH] = jnp.zeros((DEPTH - 1, 2, n, w), F32)
        else:
            ref[0] = p_ref[:, ck:ck + w]
            ref[1] = p_ref[:, cv:cv + w]

    k2 = blk(C_SK)
    v2 = blk(C_SV)
    kd = [_double_head(k2, lo if kvh == 0 else lane_hi).astype(BF16) for kvh in range(2)]
    s_mha = [_qk(_stack_halves(blk(C_NAQ + hp * LANES) * Q_SCALE, lo),
                 blk(C_NAK + hp * LANES).astype(BF16)) for hp in range(GW // LANES)]
    s_gqa = [_qk(_stack_halves(blk(C_SQ + g * LANES) * Q_SCALE, lo), kd[g // 2])
             for g in range(GW // LANES)]
    s_dif = [_qk(_stack_halves(blk(C_DQ + h * LANES) * Q_SCALE, lo),
                 blk(C_DK + h * LANES).astype(BF16)) for h in range(GW // LANES)]

    for hp, s in enumerate(s_mha):
        e = jnp.exp2(s - _row_max(s)).astype(BF16)
        r = _pv(e, _with_ones(blk(C_NAV + hp * LANES)))
        o_ref[:, hp * LANES:(hp + 1) * LANES] = _unstack_halves(_wide_div(r), lo).astype(BF16)

    for cb in range(GW // LANES):
        c = cb * LANES
        y = _conv3(blk(C_U + c), blk(C_GB + c), blk(C_GC + c), cw_ref[:, c:c + LANES])
        o_ref[:, GW + c:GW + c + LANES] = y.astype(BF16)

    sink_tab = sink_ref[...]
    v1d = [_with_ones(_double_head(v2, lo if kvh == 0 else lane_hi)) for kvh in range(2)]
    for g, s in enumerate(s_gqa):
        sink = _pair_column(n, _pick_row(sink_tab, 2 * g), _pick_row(sink_tab, 2 * g + 1)) * LOG2E
        m = jnp.maximum(_row_max(s), sink)
        e = jnp.exp2(s - m).astype(BF16)
        r = _pv(e, v1d[g // 2])
        col = 2 * GW + g * LANES
        o_ref[:, col:col + LANES] = _unstack_halves(_wide_div(r, jnp.exp2(sink - m)), lo).astype(BF16)

    lam = _lambda(lam_ref, lam_init)
    dg = dg_ref[...]
    for h, s in enumerate(s_dif):
        o = _diff_finish(s[0:n], s[n:2 * n], blk(C_DV + h * LANES).astype(BF16), lam, dg, lam_init)
        col = 3 * GW + h * LANES
        o_ref[:, col:col + LANES] = o.astype(BF16)


def _ctx_mixer(l, p, conv_w, sink_tab, diff_lambda, diff_g, lam_init, kv_prev):
    first = not kv_prev
    kv_shapes = [(N_CTX_B, DEPTH, 2, CTX_S, w) for w in (GW, LANES, GW)]
    if first:
        kv_spec = lambda w: pl.BlockSpec((None, DEPTH, 2, CTX_S, w), lambda b: (b, 0, 0, 0, 0))
    else:
        kv_spec = lambda w: pl.BlockSpec((None, None, 2, CTX_S, w), lambda b: (b, l, 0, 0, 0))
    n_in = 5
    aliases = {n_in + k: 1 + k for k in range(len(kv_prev))}
    return pl.pallas_call(
        functools.partial(_ctx_kernel, lam_init=lam_init, first_layer=first),
        grid=(N_CTX_B,),
        in_specs=[
            pl.BlockSpec((CTX_S, IN_COLS), lambda b: (b, 0)),
            pl.BlockSpec((None, 3, GW), lambda b: (l, 0, 0)),
            pl.BlockSpec((None, 8, LANES), lambda b: (l, 0, 0)),
            pl.BlockSpec((None, 4, HD), lambda b: (l, 0, 0)),
            pl.BlockSpec((None, 1, LANES), lambda b: (l, 0, 0)),
        ] + [pl.BlockSpec(memory_space=pl.ANY)] * len(kv_prev),
        out_specs=[pl.BlockSpec((CTX_S, D), lambda b: (b, 0)), kv_spec(GW), kv_spec(LANES), kv_spec(GW)],
        out_shape=[jax.ShapeDtypeStruct((CTX_ROWS, D), BF16)]
        + [jax.ShapeDtypeStruct(s, F32) for s in kv_shapes],
        input_output_aliases=aliases,
        compiler_params=_cp(("parallel",)),
        name="ctx_mixer",
    )(p, conv_w, sink_tab, diff_lambda, diff_g, *kv_prev)


NA_GROUP = 4
NA_KEYS = NA_R * GRID_W


def _na_build_bias(rpb_ref, bt_ref):
    n_dr = 2 * NA_R - 1
    for hh in range(2):
        rows = slice(hh * GRID_W, (hh + 1) * GRID_W)
        for dr in range(n_dr):
            row = jnp.broadcast_to(rpb_ref[hh, dr:dr + 1, :] * LOG2E, (GRID_W, LANES))
            t_lo = pltpu.roll(row, LANES - (NA_C - 1), 1, stride=1, stride_axis=0)
            t_hi = pltpu.roll(row, HD - (NA_C - 1), 1, stride=1, stride_axis=0)
            for j in range(NA_R):
                d0 = dr - j
                if 0 <= d0 < NA_R:
                    c = (j // 2) * LANES
                    if j % 2 == 0:
                        bt_ref[d0, rows, c:c + HD] = t_lo[:, 0:HD]
                    else:
                        bt_ref[d0, rows, c + HD:c + LANES] = t_hi[:, HD:LANES]


def _na_kernel(q_ref, k_ref, v_ref, kc_ref, vc_ref, rpb_ref, o_ref, kb_ref, v1_ref, bt_ref):
    _na_build_bias(rpb_ref, bt_ref)
    kb_ref[...] = k_ref[...].astype(BF16)
    v1_ref[...] = _with_ones(v_ref[...])
    kcb = kc_ref[...].astype(BF16)
    v1c = _with_ones(vc_ref[...])
    lo = _lane_lo(GRID_W)
    qc = lax.broadcasted_iota(jnp.int32, (2 * GRID_W, NA_KEYS), 0) % GRID_W
    kc = lax.broadcasted_iota(jnp.int32, (2 * GRID_W, NA_KEYS), 1) % GRID_W
    cs = jnp.clip(qc - NA_C // 2, 0, GRID_W - NA_C)
    col_ok = (kc >= cs) & (kc < cs + NA_C)

    def group(g, carry):
        scores = []
        for a in range(NA_GROUP):
            r = g * NA_GROUP + a
            rs = jnp.clip(r - NA_R // 2, 0, GRID_H - NA_R)
            d0 = rs - r + NA_R - 1
            q = q_ref[pl.ds(pl.multiple_of(r * GRID_W, GRID_W), GRID_W), :] * Q_SCALE
            qq = _stack_halves(q, lo)
            kw = kb_ref[pl.ds(pl.multiple_of(rs * GRID_W, GRID_W), NA_KEYS), :]
            s_loc = jnp.where(col_ok, _qk(qq, kw) + bt_ref[d0], NEG)
            scores.append((r, rs, s_loc, _qk(qq, kcb)))
        for r, rs, s_loc, s_ctx in scores:
            m = _row_max(s_loc, s_ctx)
            e_loc = jnp.exp2(s_loc - m).astype(BF16)
            e_ctx = jnp.exp2(s_ctx - m).astype(BF16)
            v1w = v1_ref[pl.ds(pl.multiple_of(rs * GRID_W, GRID_W), NA_KEYS), :]
            o = _wide_div(_pv(e_loc, v1w) + _pv(e_ctx, v1c))
            o_ref[pl.ds(pl.multiple_of(r * GRID_W, GRID_W), GRID_W), :] = (
                _unstack_halves(o, lo).astype(BF16))
        return carry

    lax.fori_loop(0, GRID_H // NA_GROUP, group, 0)


def _na(l, p, cache_na, rpb_pad):
    lat_blk = CTX_ROWS // LAT_T
    cache_spec = lambda which: pl.BlockSpec(
        (None, None, None, CTX_S, LANES), lambda b, hp: (b, l, which, 0, hp))
    return pl.pallas_call(
        _na_kernel,
        grid=(N_LAT_B, GW // LANES),
        in_specs=[
            pl.BlockSpec((LAT_T, LANES), lambda b, hp: (lat_blk + b, C_NAQ // LANES + hp)),
            pl.BlockSpec((LAT_T, LANES), lambda b, hp: (lat_blk + b, C_NAK // LANES + hp)),
            pl.BlockSpec((LAT_T, LANES), lambda b, hp: (lat_blk + b, C_NAV // LANES + hp)),
            cache_spec(0),
            cache_spec(1),
            pl.BlockSpec((None, 2, 2 * NA_R - 1, LANES), lambda b, hp: (l, hp, 0, 0)),
        ],
        out_specs=pl.BlockSpec((LAT_T, LANES), lambda b, hp: (b, hp)),
        out_shape=jax.ShapeDtypeStruct((LAT_ROWS, GW), BF16),
        scratch_shapes=[pltpu.VMEM((LAT_T, LANES), BF16), pltpu.VMEM((LAT_T, 2 * LANES), BF16),
                        pltpu.VMEM((NA_R, 2 * GRID_W, NA_KEYS), F32)],
        compiler_params=_cp(("parallel", "parallel")),
        name="na_latent",
    )(p, p, p, cache_na, cache_na, rpb_pad)


def _conv_kernel(u_ref, gb_ref, gc_ref, w_ref, o_ref):
    o_ref[...] = _conv3(u_ref[...], gb_ref[...], gc_ref[...], w_ref[...]).astype(BF16)


def _conv_latent(l, p, conv_w):
    lat_blk = CTX_ROWS // LAT_T
    spec = lambda c0: pl.BlockSpec((LAT_T, LANES), lambda b, c: (lat_blk + b, c0 // LANES + c))
    return pl.pallas_call(
        _conv_kernel,
        grid=(N_LAT_B, GW // LANES),
        in_specs=[spec(C_U), spec(C_GB), spec(C_GC),
                  pl.BlockSpec((None, 3, LANES), lambda b, c: (l, 0, c))],
        out_specs=pl.BlockSpec((LAT_T, LANES), lambda b, c: (b, c)),
        out_shape=jax.ShapeDtypeStruct((LAT_ROWS, GW), BF16),
        compiler_params=_cp(("parallel", "parallel")),
        name="conv_latent",
    )(p, p, p, conv_w)


SWA_BLK = 128
SWA_KEYS = SWA_BLK + 2 * SWA_W
SWA_PAD = LAT_T + 2 * SWA_W


def _swa_kernel(q_ref, k_ref, v_ref, kc_ref, vc_ref, cos_ref, sin_ref, sink_ref, o_ref,
                kd_ref, v1_ref, kcd_ref, v1c_ref):
    n = pl.program_id(1)
    i = pl.program_id(2)

    @pl.when(i == 0)
    def _():
        own = jnp.logical_not(_lane_lo(LAT_T)).astype(jnp.int32) == n
        own_c = jnp.logical_not(_lane_lo(CTX_S)).astype(jnp.int32) == n
        kr = _rope(k_ref[...], cos_ref[...], sin_ref[...])
        kd_ref[0:SWA_W, :] = jnp.zeros((SWA_W, LANES), BF16)
        kd_ref[SWA_W + LAT_T:SWA_PAD, :] = jnp.zeros((SWA_W, LANES), BF16)
        kd_ref[SWA_W:SWA_W + LAT_T, :] = _double_head(kr, own).astype(BF16)
        kcd_ref[...] = _double_head(kc_ref[...], own_c).astype(BF16)
        v1_ref[0:SWA_W, :] = jnp.zeros((SWA_W, 2 * LANES), BF16)
        v1_ref[SWA_W + LAT_T:SWA_PAD, :] = jnp.zeros((SWA_W, 2 * LANES), BF16)
        v1_ref[SWA_W:SWA_W + LAT_T, :] = _with_ones(_double_head(v_ref[...], own))
        v1c_ref[...] = _with_ones(_double_head(vc_ref[...], own_c))

    lo = _lane_lo(SWA_BLK)
    start = pl.multiple_of(i * SWA_BLK, SWA_BLK)
    cos = cos_ref[pl.ds(start, SWA_BLK), :]
    sin = sin_ref[pl.ds(start, SWA_BLK), :]
    kw = kd_ref[pl.ds(start, SWA_KEYS), :]
    kcb = kcd_ref[...]
    a = lax.broadcasted_iota(jnp.int32, (2 * SWA_BLK, SWA_KEYS), 0) % SWA_BLK
    j = lax.broadcasted_iota(jnp.int32, (2 * SWA_BLK, SWA_KEYS), 1)
    kpos = start - SWA_W + j
    valid = (j - a >= 0) & (j - a <= 2 * SWA_W) & (kpos >= 0) & (kpos < LAT_T)
    sink_tab = sink_ref[...]
    scores = []
    for c in range(2):
        qq = _stack_halves(_rope(q_ref[:, c * LANES:(c + 1) * LANES], cos, sin) * Q_SCALE, lo)
        scores.append((jnp.where(valid, _qk(qq, kw), NEG), _qk(qq, kcb)))
    for c, (s_loc, s_ctx) in enumerate(scores):
        h = 4 * n + 2 * c
        sink = _pair_column(SWA_BLK, _pick_row(sink_tab, h), _pick_row(sink_tab, h + 1)) * LOG2E
        m = jnp.maximum(_row_max(s_loc, s_ctx), sink)
        e_loc = jnp.exp2(s_loc - m).astype(BF16)
        e_ctx = jnp.exp2(s_ctx - m).astype(BF16)
        r = _pv(e_loc, v1_ref[pl.ds(start, SWA_KEYS), :]) + _pv(e_ctx, v1c_ref[...])
        o = _wide_div(r, jnp.exp2(sink - m))
        o_ref[:, c * LANES:(c + 1) * LANES] = _unstack_halves(o, lo).astype(BF16)


def _swa(l, p, cache_swa, cos_tab, sin_tab, sink_tab):
    nq = LAT_T // SWA_BLK
    lat_blk = CTX_ROWS // LAT_T
    cache_spec = lambda which: pl.BlockSpec(
        (None, None, None, CTX_S, LANES), lambda b, n, i: (b, l, which, 0, 0))
    full = pl.BlockSpec((LAT_T, LANES), lambda b, n, i: (0, 0))
    return pl.pallas_call(
        _swa_kernel,
        grid=(N_LAT_B, 2, nq),
        in_specs=[
            pl.BlockSpec((SWA_BLK, 2 * LANES),
                         lambda b, n, i: (CTX_ROWS // SWA_BLK + b * nq + i, C_SQ // (2 * LANES) + n)),
            pl.BlockSpec((LAT_T, LANES), lambda b, n, i: (lat_blk + b, C_SK // LANES)),
            pl.BlockSpec((LAT_T, LANES), lambda b, n, i: (lat_blk + b, C_SV // LANES)),
            cache_spec(0),
            cache_spec(1),
            full,
            full,
            pl.BlockSpec((None, 8, LANES), lambda b, n, i: (l, 0, 0)),
        ],
        out_specs=pl.BlockSpec((SWA_BLK, 2 * LANES), lambda b, n, i: (b * nq + i, n)),
        out_shape=jax.ShapeDtypeStruct((LAT_ROWS, GW), BF16),
        scratch_shapes=[pltpu.VMEM((SWA_PAD, LANES), BF16), pltpu.VMEM((SWA_PAD, 2 * LANES), BF16),
                        pltpu.VMEM((CTX_S, LANES), BF16), pltpu.VMEM((CTX_S, 2 * LANES), BF16)],
        compiler_params=_cp(("parallel", "parallel", "arbitrary")),
        name="swa_latent",
    )(p, p, p, cache_swa, cache_swa, cos_tab, sin_tab, sink_tab)


DIFF_TQ = 256
DIFF_KEYS = LAT_T + CTX_S


def _diff_kernel(q_ref, k_ref, v_ref, kc_ref, vc_ref, cos_ref, sin_ref, lam_ref, dg_ref, o_ref,
                 kb_ref, vb_ref, *, lam_init):
    i = pl.program_id(2)

    @pl.when(i == 0)
    def _():
        kb_ref[0:LAT_T, :] = _rope(k_ref[...], cos_ref[...], sin_ref[...]).astype(BF16)
        kb_ref[LAT_T:DIFF_KEYS, :] = kc_ref[...].astype(BF16)
        vb_ref[0:LAT_T, :] = v_ref[...].astype(BF16)
        vb_ref[LAT_T:DIFF_KEYS, :] = vc_ref[...].astype(BF16)

    start = pl.multiple_of(i * DIFF_TQ, DIFF_TQ)
    cos = cos_ref[pl.ds(start, DIFF_TQ), :]
    sin = sin_ref[pl.ds(start, DIFF_TQ), :]
    q = _rope(q_ref[...], cos, sin) * Q_SCALE
    lo = _lane_lo(DIFF_TQ)
    s1 = _qk(jnp.where(lo, q, 0.0).astype(BF16), kb_ref[...])
    s2 = _qk(jnp.where(lo, 0.0, q).astype(BF16), kb_ref[...])
    lam = _lambda(lam_ref, lam_init)
    o_ref[...] = _diff_finish(s1, s2, vb_ref[...], lam, dg_ref[...], lam_init).astype(BF16)


def _diff_latent(l, p, cache_diff, cos_tab, sin_tab, diff_lambda, diff_g, lam_init):
    nq = LAT_T // DIFF_TQ
    lat_blk = CTX_ROWS // LAT_T
    cache_spec = lambda which: pl.BlockSpec(
        (None, None, None, CTX_S, LANES), lambda b, h, i: (b, l, which, 0, h))
    full = pl.BlockSpec((LAT_T, LANES), lambda b, h, i: (0, 0))
    return pl.pallas_call(
        functools.partial(_diff_kernel, lam_init=lam_init),
        grid=(N_LAT_B, GW // LANES, nq),
        in_specs=[
            pl.BlockSpec((DIFF_TQ, LANES),
                         lambda b, h, i: (CTX_ROWS // DIFF_TQ + b * nq + i, C_DQ // LANES + h)),
            pl.BlockSpec((LAT_T, LANES), lambda b, h, i: (lat_blk + b, C_DK // LANES + h)),
            pl.BlockSpec((LAT_T, LANES), lambda b, h, i: (lat_blk + b, C_DV // LANES + h)),
            cache_spec(0),
            cache_spec(1),
            full,
            full,
            pl.BlockSpec((None, 4, HD), lambda b, h, i: (l, 0, 0)),
            pl.BlockSpec((None, 1, LANES), lambda b, h, i: (l, 0, 0)),
        ],
        out_specs=pl.BlockSpec((DIFF_TQ, LANES), lambda b, h, i: (b * nq + i, h)),
        out_shape=jax.ShapeDtypeStruct((LAT_ROWS, GW), BF16),
        scratch_shapes=[pltpu.VMEM((DIFF_KEYS, LANES), BF16), pltpu.VMEM((DIFF_KEYS, LANES), BF16)],
        compiler_params=_cp(("parallel", "parallel", "arbitrary")),
        name="diff_latent",
    )(p, p, p, cache_diff, cache_diff, cos_tab, sin_tab, diff_lambda, diff_g)


OUTPROJ_SPLIT = 2


def _outproj_kernel(mc_ref, oa_ref, ob_ref, oc_ref, od_ref, x_ref, g_ref, m_ref, w_ref,
                    x1_ref, h2_ref, *, n_ctx_blocks):
    i = pl.program_id(0)
    tm = x_ref.shape[0]
    sub = tm // OUTPROJ_SPLIT

    def finish(rows, y):
        x1 = x_ref[rows, :] + m_ref[2:3, :] * (_rms(y) * g_ref[1:2, :])
        x1_ref[rows, :] = x1
        h2 = (_rms(x1) * g_ref[2:3, :]) * (1.0 + m_ref[4:5, :]) + m_ref[3:4, :]
        h2_ref[rows, :] = h2.astype(BF16)

    @pl.when(i < n_ctx_blocks)
    def _():
        for s in range(OUTPROJ_SPLIT):
            rows = slice(s * sub, (s + 1) * sub)
            finish(rows, jnp.dot(mc_ref[rows, :], w_ref[...], preferred_element_type=F32))

    @pl.when(i >= n_ctx_blocks)
    def _():
        for s in range(OUTPROJ_SPLIT):
            rows = slice(s * sub, (s + 1) * sub)
            mix = jnp.concatenate([oa_ref[rows, :], ob_ref[rows, :], oc_ref[rows, :], od_ref[rows, :]],
                                  axis=1)
            finish(rows, jnp.dot(mix, w_ref[...], preferred_element_type=F32))


def _outproj(l, mix_ctx, lat_parts, x, norm_g, mods, w_out):
    tm = 512
    nc = CTX_ROWS // tm
    lat_spec = pl.BlockSpec((tm, GW), lambda i: (jnp.maximum(i - nc, 0), 0))
    return pl.pallas_call(
        functools.partial(_outproj_kernel, n_ctx_blocks=nc),
        grid=(ROWS // tm,),
        in_specs=[
            pl.BlockSpec((tm, D), lambda i: (jnp.minimum(i, nc - 1), 0)),
            lat_spec, lat_spec, lat_spec, lat_spec,
            pl.BlockSpec((tm, D), lambda i: (i, 0)),
            pl.BlockSpec((None, 4, D), lambda i: (l, 0, 0)),
            pl.BlockSpec((None, None, N_MOD, D), lambda i: (l, _row_group(i, tm), 0, 0)),
            pl.BlockSpec((None, D, D), lambda i: (l, 0, 0)),
        ],
        out_specs=[pl.BlockSpec((tm, D), lambda i: (i, 0)),
                   pl.BlockSpec((tm, D), lambda i: (i, 0))],
        out_shape=[jax.ShapeDtypeStruct((ROWS, D), F32), jax.ShapeDtypeStruct((ROWS, D), BF16)],
        compiler_params=_cp(("parallel",)),
        name="outproj",
    )(mix_ctx, *lat_parts, x, norm_g, mods, w_out)


def _mlp_kernel(h_ref, x1_ref, g_ref, m_ref, w1_ref, w2_ref, o_ref, acc_ref):
    j = pl.program_id(1)
    a = jnp.dot(h_ref[...], w1_ref[...], preferred_element_type=F32)
    a = jnp.square(jnp.maximum(a, 0.0)).astype(BF16)
    part = jnp.dot(a, w2_ref[...], preferred_element_type=F32)

    @pl.when(j == 0)
    def _():
        acc_ref[...] = part

    @pl.when(j > 0)
    def _():
        acc_ref[...] += part

    @pl.when(j == pl.num_programs(1) - 1)
    def _():
        o_ref[...] = x1_ref[...] + m_ref[5:6, :] * (_rms(acc_ref[...]) * g_ref[3:4, :])


def _mlp(l, h2, x1, norm_g, mods, w1, w2):
    tm, th = 512, 1024
    return pl.pallas_call(
        _mlp_kernel,
        grid=(ROWS // tm, HIDDEN // th),
        in_specs=[
            pl.BlockSpec((tm, D), lambda i, j: (i, 0)),
            pl.BlockSpec((tm, D), lambda i, j: (i, 0)),
            pl.BlockSpec((None, 4, D), lambda i, j: (l, 0, 0)),
            pl.BlockSpec((None, None, N_MOD, D), lambda i, j: (l, _row_group(i, tm), 0, 0)),
            pl.BlockSpec((None, D, th), lambda i, j: (l, 0, j)),
            pl.BlockSpec((None, th, D), lambda i, j: (l, j, 0)),
        ],
        out_specs=pl.BlockSpec((tm, D), lambda i, j: (i, 0)),
        out_shape=jax.ShapeDtypeStruct((ROWS, D), F32),
        scratch_shapes=[pltpu.VMEM((tm, D), F32)],
        compiler_params=_cp(("parallel", "arbitrary")),
        name="mlp",
    )(h2, x1, norm_g, mods, w1, w2)


def _rope_tables():
    t = np.arange(LAT_T)
    rows = (t // GRID_W).astype(np.float32)
    cols = (t % GRID_W).astype(np.float32)
    n = HD // 4
    inv = jnp.asarray(10000.0, F32) ** (-jnp.arange(n, dtype=F32) / n)
    ar = jnp.asarray(rows)[:, None] * inv[None]
    ac = jnp.asarray(cols)[:, None] * inv[None]
    cos = jnp.concatenate([jnp.cos(ar), jnp.cos(ar), jnp.cos(ac), jnp.cos(ac)], axis=-1)
    sin = jnp.concatenate([-jnp.sin(ar), jnp.sin(ar), -jnp.sin(ac), jnp.sin(ac)], axis=-1)
    return jnp.tile(cos, (1, 2)), jnp.tile(sin, (1, 2))


def kernel(x_prompt, x_sample, cache_na_kv, cache_swa_kv, cache_diff_kv, c, c_ctx, ada_w, ada_b,
           norm_g, w_in, conv_w, na_rpb, swa_sink, diff_lambda, diff_norm_g, w_out, mlp_w1, mlp_w2):
    cvec = jnp.concatenate([c_ctx[None], c, jnp.zeros((8 - 1 - N_LAT_B, D), F32)], axis=0)
    mods = _modulation(cvec, ada_w, ada_b).reshape(DEPTH, 8, N_MOD, D)

    w_out_b = w_out.astype(BF16)
    w1_b = mlp_w1.astype(BF16)
    w2_b = mlp_w2.astype(BF16)

    cache_na = cache_na_kv.reshape(N_LAT_B, DEPTH, 2, CTX_S, GW)
    cache_swa = cache_swa_kv.reshape(N_LAT_B, DEPTH, 2, CTX_S, LANES)
    cache_diff = cache_diff_kv.reshape(N_LAT_B, DEPTH, 2, CTX_S, GW)
    cos_tab, sin_tab = _rope_tables()
    sink_tab = jnp.broadcast_to(swa_sink[:, :, None], (DEPTH, 8, LANES))
    diff_g = diff_norm_g.reshape(DEPTH, 1, LANES)

    rpb_pad = jnp.pad(na_rpb, ((0, 0), (0, 0), (0, 0), (0, LANES - na_rpb.shape[-1])))

    x = jnp.concatenate([x_prompt.reshape(CTX_ROWS, D), x_sample.reshape(LAT_ROWS, D)], axis=0)
    kv = []
    for l in range(DEPTH):
        lam_init = 0.8 - 0.6 * math.exp(-0.3 * l)
        p = _inproj(l, x, norm_g, mods, w_in)
        mix_ctx, *kv = _ctx_mixer(l, p, conv_w, sink_tab, diff_lambda, diff_g, lam_init, kv)
        o_a = _na(l, p, cache_na, rpb_pad)
        o_b = _conv_latent(l, p, conv_w)
        o_c = _swa(l, p, cache_swa, cos_tab, sin_tab, sink_tab)
        o_d = _diff_latent(l, p, cache_diff, cos_tab, sin_tab, diff_lambda, diff_g, lam_init)
        x1, h2 = _outproj(l, mix_ctx, (o_a, o_b, o_c, o_d), x, norm_g, mods, w_out_b)
        x = _mlp(l, h2, x1, norm_g, mods, w1_b, w2_b)

    na_kv, swa_kv, diff_kv = kv
    return (x[:CTX_ROWS].reshape(N_CTX_B, CTX_S, D),
            x[CTX_ROWS:].reshape(N_LAT_B, LAT_T, D),
            na_kv.reshape(N_CTX_B, DEPTH, 2, CTX_S, 8, HD),
            swa_kv.reshape(N_CTX_B, DEPTH, 2, CTX_S, 2, HD),
            diff_kv.reshape(N_CTX_B, DEPTH, 2, CTX_S, 4, 2 * HD))
```

```python
import functools
import math

import numpy as np
import jax
import jax.numpy as jnp
from jax import lax
from jax.experimental import pallas as pl
from jax.experimental.pallas import tpu as pltpu

F32 = jnp.float32
BF16 = jnp.bfloat16

D = 2048
N_CTX_B, CTX_S = 16, 256
N_LAT_B, LAT_T = 2, 2048
CTX_ROWS = N_CTX_B * CTX_S
LAT_ROWS = N_LAT_B * LAT_T
ROWS = CTX_ROWS + LAT_ROWS
DEPTH = 4
GRID_W = 64
GRID_H = LAT_T // GRID_W
HD = 64
GW = 512
IN_COLS = 5376
HIDDEN = 4 * D
N_MOD = 6
EPS = 1e-6
NEG = -1e30
SCALE = HD ** -0.5
LOG2E = math.log2(math.e)
Q_SCALE = SCALE * LOG2E
NA_R, NA_C = 8, 16
SWA_W = 128
LANES = 128

C_NAQ, C_NAK, C_NAV = 0, 512, 1024
C_U, C_GB, C_GC = 1536, 2048, 2560
C_SQ, C_SK, C_SV = 3072, 3584, 3712
C_DQ, C_DK, C_DV = 3840, 4352, 4864

VMEM_LIMIT = 56 * 1024 * 1024


def _cp(sem):
    return pltpu.CompilerParams(dimension_semantics=sem, vmem_limit_bytes=VMEM_LIMIT)


def _rms(x):
    return x * lax.rsqrt(jnp.mean(x * x, axis=-1, keepdims=True) + EPS)


def _qk(q, k):
    return lax.dot_general(q, k, (((1,), (1,)), ((), ())), preferred_element_type=F32)


def _pv(p, v):
    return jnp.dot(p, v, preferred_element_type=F32)


def _lane_lo(rows):
    return lax.broadcasted_iota(jnp.int32, (rows, LANES), 1) < HD


def _row_group(i, tm):
    return jnp.maximum(i * tm - LAT_T, 0) // LAT_T


def _mod_kernel(c_ref, w_ref, b_ref, o_ref):
    cv = c_ref[...]
    s = cv * (1.0 / (1.0 + jnp.exp(-cv)))
    o_ref[...] = jnp.dot(s.astype(BF16), w_ref[...].astype(BF16),
                         preferred_element_type=F32) + b_ref[...]


def _modulation(cvec, ada_w, ada_b):
    tn = 1024
    n = N_MOD * D
    return pl.pallas_call(
        _mod_kernel,
        grid=(DEPTH, n // tn),
        in_specs=[
            pl.BlockSpec((8, D), lambda l, j: (0, 0)),
            pl.BlockSpec((None, D, tn), lambda l, j: (l, 0, j)),
            pl.BlockSpec((None, 1, tn), lambda l, j: (l, 0, j)),
        ],
        out_specs=pl.BlockSpec((None, 8, tn), lambda l, j: (l, 0, j)),
        out_shape=jax.ShapeDtypeStruct((DEPTH, 8, n), F32),
        compiler_params=_cp(("parallel", "parallel")),
        name="modulation",
    )(cvec, ada_w, ada_b.reshape(DEPTH, 1, n))


INPROJ_SPLIT = 4


def _inproj_kernel(x_ref, g_ref, m_ref, w_ref, o_ref, h_ref):
    j = pl.program_id(1)
    wb = w_ref[...].astype(BF16)

    @pl.when(j == 0)
    def _():
        sub = x_ref.shape[0] // INPROJ_SPLIT
        for s in range(INPROJ_SPLIT):
            rows = slice(s * sub, (s + 1) * sub)
            h = _rms(x_ref[rows, :]) * g_ref[0:1, :]
            h = (h * (1.0 + m_ref[1:2, :]) + m_ref[0:1, :]).astype(BF16)
            h_ref[rows, :] = h
            o_ref[rows, :] = jnp.dot(h, wb, preferred_element_type=F32)

    @pl.when(j > 0)
    def _():
        o_ref[...] = jnp.dot(h_ref[...], wb, preferred_element_type=F32)


def _inproj(l, x, norm_g, mods, w_in):
    tm, tn = 1024, 768
    return pl.pallas_call(
        _inproj_kernel,
        grid=(ROWS // tm, IN_COLS // tn),
        in_specs=[
            pl.BlockSpec((tm, D), lambda i, j: (i, 0)),
            pl.BlockSpec((None, 4, D), lambda i, j: (l, 0, 0)),
            pl.BlockSpec((None, None, N_MOD, D), lambda i, j: (l, _row_group(i, tm), 0, 0)),
            pl.BlockSpec((None, D, tn), lambda i, j: (l, 0, j)),
        ],
        out_specs=pl.BlockSpec((tm, tn), lambda i, j: (i, j)),
        out_shape=jax.ShapeDtypeStruct((ROWS, IN_COLS), F32),
        scratch_shapes=[pltpu.VMEM((tm, D), BF16)],
        compiler_params=_cp(("parallel", "arbitrary")),
        name="inproj",
    )(x, norm_g, mods, w_in)


def _conv3(u, gb, gc, w):
    z = gc * u
    n = z.shape[0]
    row = lax.broadcasted_iota(jnp.int32, z.shape, 0)
    zp = jnp.where(row == 0, 0.0, pltpu.roll(z, 1, 0))
    zn = jnp.where(row == n - 1, 0.0, pltpu.roll(z, n - 1, 0))
    return gb * (zp * w[0:1, :] + z * w[1:2, :] + zn * w[2:3, :])


def _lambda(lam_ref, lam_init):
    lp = lam_ref[...]
    a = jnp.sum(lp[0:1, :] * lp[1:2, :], axis=-1, keepdims=True)
    b = jnp.sum(lp[2:3, :] * lp[3:4, :], axis=-1, keepdims=True)
    return jnp.exp(a) - jnp.exp(b) + lam_init


def _stack_halves(q, lo):
    return jnp.concatenate([jnp.where(lo, q, 0.0), jnp.where(lo, 0.0, q)], axis=0).astype(BF16)


def _unstack_halves(o, lo):
    n = o.shape[0] // 2
    return jnp.where(lo, o[0:n], o[n:2 * n])


def _with_ones(v):
    return jnp.concatenate([v, jnp.ones_like(v)], axis=1).astype(BF16)


def _row_max(*parts):
    m = jnp.max(parts[0], axis=-1, keepdims=True)
    for p in parts[1:]:
        m = jnp.maximum(m, jnp.max(p, axis=-1, keepdims=True))
    return m


def _wide_div(r, extra=None):
    den = r[:, LANES:2 * LANES]
    return r[:, 0:LANES] / (den if extra is None else den + extra)


def _pair_column(n, first, second):
    row = lax.broadcasted_iota(jnp.int32, (2 * n, 1), 0)
    return jnp.where(row < n, first, second)


def _diff_finish(s1, s2, vb, lam, dg, lam_init):
    e1 = jnp.exp2(s1 - jnp.max(s1, axis=-1, keepdims=True))
    e2 = jnp.exp2(s2 - jnp.max(s2, axis=-1, keepdims=True))
    sum1 = jnp.sum(e1, axis=-1, keepdims=True)
    c = lam * sum1 / jnp.sum(e2, axis=-1, keepdims=True)
    o = _pv((e1 - c * e2).astype(BF16), vb) / sum1
    return (_rms(o) * dg) * (1.0 - lam_init)


def _double_head(x, own):
    return jnp.where(own, x, pltpu.roll(x, HD, 1))


def _pick_row(tab, idx):
    row = lax.broadcasted_iota(jnp.int32, tab.shape, 0)
    return jnp.sum(jnp.where(row == idx, tab, 0.0), axis=0, keepdims=True)[:, 0:1]


def _swap16(x):
    lane = lax.broadcasted_iota(jnp.int32, x.shape, 1)
    return jnp.where(lane % 32 < 16, pltpu.roll(x, LANES - 16, 1), pltpu.roll(x, 16, 1))


def _rope(x, cos, sin):
    return x * cos + _swap16(x) * sin


def _ctx_kernel(p_ref, cw_ref, sink_ref, lam_ref, dg_ref, *rest, lam_init, first_layer):
    o_ref, na_ref, swa_ref, df_ref = rest[-4:]
    n = CTX_S
    lo = _lane_lo(n)
    lane_hi = jnp.logical_not(lo)

    def blk(c):
        return p_ref[:, c:c + LANES]

    kv_parts = ((na_ref, C_NAK, C_NAV, GW), (swa_ref, C_SK, C_SV, LANES), (df_ref, C_DK, C_DV, GW))
    for ref, ck, cv, w in kv_parts:
        if first_layer:
            ref[0, 0] = p_ref[:, ck:ck + w]
            ref[0, 1] = p_ref[:, cv:cv + w]
            ref[1:DEPTH] = jnp.zeros((DEPTH - 1, 2, n, w), F32)
        else:
            ref[0] = p_ref[:, ck:ck + w]
            ref[1] = p_ref[:, cv:cv + w]

    k2 = blk(C_SK)
    v2 = blk(C_SV)
    kd = [_double_head(k2, lo if kvh == 0 else lane_hi).astype(BF16) for kvh in range(2)]
    s_mha = [_qk(_stack_halves(blk(C_NAQ + hp * LANES) * Q_SCALE, lo),
                 blk(C_NAK + hp * LANES).astype(BF16)) for hp in range(GW // LANES)]
    s_gqa = [_qk(_stack_halves(blk(C_SQ + g * LANES) * Q_SCALE, lo), kd[g // 2])
             for g in range(GW // LANES)]
    s_dif = [_qk(_stack_halves(blk(C_DQ + h * LANES) * Q_SCALE, lo),
                 blk(C_DK + h * LANES).astype(BF16)) for h in range(GW // LANES)]

    for hp, s in enumerate(s_mha):
        e = jnp.exp2(s - _row_max(s)).astype(BF16)
        r = _pv(e, _with_ones(blk(C_NAV + hp * LANES)))
        o_ref[:, hp * LANES:(hp + 1) * LANES] = _unstack_halves(_wide_div(r), lo).astype(BF16)

    for cb in range(GW // LANES):
        c = cb * LANES
        y = _conv3(blk(C_U + c), blk(C_GB + c), blk(C_GC + c), cw_ref[:, c:c + LANES])
        o_ref[:, GW + c:GW + c + LANES] = y.astype(BF16)

    sink_tab = sink_ref[...]
    v1d = [_with_ones(_double_head(v2, lo if kvh == 0 else lane_hi)) for kvh in range(2)]
    for g, s in enumerate(s_gqa):
        sink = _pair_column(n, _pick_row(sink_tab, 2 * g), _pick_row(sink_tab, 2 * g + 1)) * LOG2E
        m = jnp.maximum(_row_max(s), sink)
        e = jnp.exp2(s - m).astype(BF16)
        r = _pv(e, v1d[g // 2])
        col = 2 * GW + g * LANES
        o_ref[:, col:col + LANES] = _unstack_halves(_wide_div(r, jnp.exp2(sink - m)), lo).astype(BF16)

    lam = _lambda(lam_ref, lam_init)
    dg = dg_ref[...]
    for h, s in enumerate(s_dif):
        o = _diff_finish(s[0:n], s[n:2 * n], blk(C_DV + h * LANES).astype(BF16), lam, dg, lam_init)
        col = 3 * GW + h * LANES
        o_ref[:, col:col + LANES] = o.astype(BF16)


def _ctx_mixer(l, p, conv_w, sink_tab, diff_lambda, diff_g, lam_init, kv_prev):
    first = not kv_prev
    kv_shapes = [(N_CTX_B, DEPTH, 2, CTX_S, w) for w in (GW, LANES, GW)]
    if first:
        kv_spec = lambda w: pl.BlockSpec((None, DEPTH, 2, CTX_S, w), lambda b: (b, 0, 0, 0, 0))
    else:
        kv_spec = lambda w: pl.BlockSpec((None, None, 2, CTX_S, w), lambda b: (b, l, 0, 0, 0))
    n_in = 5
    aliases = {n_in + k: 1 + k for k in range(len(kv_prev))}
    return pl.pallas_call(
        functools.partial(_ctx_kernel, lam_init=lam_init, first_layer=first),
        grid=(N_CTX_B,),
        in_specs=[
            pl.BlockSpec((CTX_S, IN_COLS), lambda b: (b, 0)),
            pl.BlockSpec((None, 3, GW), lambda b: (l, 0, 0)),
            pl.BlockSpec((None, 8, LANES), lambda b: (l, 0, 0)),
            pl.BlockSpec((None, 4, HD), lambda b: (l, 0, 0)),
            pl.BlockSpec((None, 1, LANES), lambda b: (l, 0, 0)),
        ] + [pl.BlockSpec(memory_space=pl.ANY)] * len(kv_prev),
        out_specs=[pl.BlockSpec((CTX_S, D), lambda b: (b, 0)), kv_spec(GW), kv_spec(LANES), kv_spec(GW)],
        out_shape=[jax.ShapeDtypeStruct((CTX_ROWS, D), BF16)]
        + [jax.ShapeDtypeStruct(s, F32) for s in kv_shapes],
        input_output_aliases=aliases,
        compiler_params=_cp(("parallel",)),
        name="ctx_mixer",
    )(p, conv_w, sink_tab, diff_lambda, diff_g, *kv_prev)


NA_GROUP = 4
NA_KEYS = NA_R * GRID_W


def _na_build_bias(rpb_ref, bt_ref):
    n_dr = 2 * NA_R - 1
    for hh in range(2):
        rows = slice(hh * GRID_W, (hh + 1) * GRID_W)
        for dr in range(n_dr):
            row = jnp.broadcast_to(rpb_ref[hh, dr:dr + 1, :] * LOG2E, (GRID_W, LANES))
            t_lo = pltpu.roll(row, LANES - (NA_C - 1), 1, stride=1, stride_axis=0)
            t_hi = pltpu.roll(row, HD - (NA_C - 1), 1, stride=1, stride_axis=0)
            for j in range(NA_R):
                d0 = dr - j
                if 0 <= d0 < NA_R:
                    c = (j // 2) * LANES
                    if j % 2 == 0:
                        bt_ref[d0, rows, c:c + HD] = t_lo[:, 0:HD]
                    else:
                        bt_ref[d0, rows, c + HD:c + LANES] = t_hi[:, HD:LANES]


def _na_kernel(q_ref, k_ref, v_ref, kc_ref, vc_ref, rpb_ref, o_ref, kb_ref, v1_ref, bt_ref):
    _na_build_bias(rpb_ref, bt_ref)
    kb_ref[...] = k_ref[...].astype(BF16)
    v1_ref[...] = _with_ones(v_ref[...])
    kcb = kc_ref[...].astype(BF16)
    v1c = _with_ones(vc_ref[...])
    lo = _lane_lo(GRID_W)
    qc = lax.broadcasted_iota(jnp.int32, (2 * GRID_W, NA_KEYS), 0) % GRID_W
    kc = lax.broadcasted_iota(jnp.int32, (2 * GRID_W, NA_KEYS), 1) % GRID_W
    cs = jnp.clip(qc - NA_C // 2, 0, GRID_W - NA_C)
    col_ok = (kc >= cs) & (kc < cs + NA_C)

    def group(g, carry):
        scores = []
        for a in range(NA_GROUP):
            r = g * NA_GROUP + a
            rs = jnp.clip(r - NA_R // 2, 0, GRID_H - NA_R)
            d0 = rs - r + NA_R - 1
            q = q_ref[pl.ds(pl.multiple_of(r * GRID_W, GRID_W), GRID_W), :] * Q_SCALE
            qq = _stack_halves(q, lo)
            kw = kb_ref[pl.ds(pl.multiple_of(rs * GRID_W, GRID_W), NA_KEYS), :]
            s_loc = jnp.where(col_ok, _qk(qq, kw) + bt_ref[d0], NEG)
            scores.append((r, rs, s_loc, _qk(qq, kcb)))
        for r, rs, s_loc, s_ctx in scores:
            m = _row_max(s_loc, s_ctx)
            e_loc = jnp.exp2(s_loc - m).astype(BF16)
            e_ctx = jnp.exp2(s_ctx - m).astype(BF16)
            v1w = v1_ref[pl.ds(pl.multiple_of(rs * GRID_W, GRID_W), NA_KEYS), :]
            o = _wide_div(_pv(e_loc, v1w) + _pv(e_ctx, v1c))
            o_ref[pl.ds(pl.multiple_of(r * GRID_W, GRID_W), GRID_W), :] = (
                _unstack_halves(o, lo).astype(BF16))
        return carry

    lax.fori_loop(0, GRID_H // NA_GROUP, group, 0)


def _na(l, p, cache_na, rpb_pad):
    lat_blk = CTX_ROWS // LAT_T
    cache_spec = lambda which: pl.BlockSpec(
        (None, None, None, CTX_S, LANES), lambda b, hp: (b, l, which, 0, hp))
    return pl.pallas_call(
        _na_kernel,
        grid=(N_LAT_B, GW // LANES),
        in_specs=[
            pl.BlockSpec((LAT_T, LANES), lambda b, hp: (lat_blk + b, C_NAQ // LANES + hp)),
            pl.BlockSpec((LAT_T, LANES), lambda b, hp: (lat_blk + b, C_NAK // LANES + hp)),
            pl.BlockSpec((LAT_T, LANES), lambda b, hp: (lat_blk + b, C_NAV // LANES + hp)),
            cache_spec(0),
            cache_spec(1),
            pl.BlockSpec((None, 2, 2 * NA_R - 1, LANES), lambda b, hp: (l, hp, 0, 0)),
        ],
        out_specs=pl.BlockSpec((LAT_T, LANES), lambda b, hp: (b, hp)),
        out_shape=jax.ShapeDtypeStruct((LAT_ROWS, GW), BF16),
        scratch_shapes=[pltpu.VMEM((LAT_T, LANES), BF16), pltpu.VMEM((LAT_T, 2 * LANES), BF16),
                        pltpu.VMEM((NA_R, 2 * GRID_W, NA_KEYS), F32)],
        compiler_params=_cp(("parallel", "parallel")),
        name="na_latent",
    )(p, p, p, cache_na, cache_na, rpb_pad)


def _conv_kernel(u_ref, gb_ref, gc_ref, w_ref, o_ref):
    o_ref[...] = _conv3(u_ref[...], gb_ref[...], gc_ref[...], w_ref[...]).astype(BF16)


def _conv_latent(l, p, conv_w):
    lat_blk = CTX_ROWS // LAT_T
    spec = lambda c0: pl.BlockSpec((LAT_T, LANES), lambda b, c: (lat_blk + b, c0 // LANES + c))
    return pl.pallas_call(
        _conv_kernel,
        grid=(N_LAT_B, GW // LANES),
        in_specs=[spec(C_U), spec(C_GB), spec(C_GC),
                  pl.BlockSpec((None, 3, LANES), lambda b, c: (l, 0, c))],
        out_specs=pl.BlockSpec((LAT_T, LANES), lambda b, c: (b, c)),
        out_shape=jax.ShapeDtypeStruct((LAT_ROWS, GW), BF16),
        compiler_params=_cp(("parallel", "parallel")),
        name="conv_latent",
    )(p, p, p, conv_w)


SWA_BLK = 128
SWA_KEYS = SWA_BLK + 2 * SWA_W
SWA_PAD = LAT_T + 2 * SWA_W


def _swa_kernel(q_ref, k_ref, v_ref, kc_ref, vc_ref, cos_ref, sin_ref, sink_ref, o_ref,
                kd_ref, v1_ref, kcd_ref, v1c_ref):
    n = pl.program_id(1)
    i = pl.program_id(2)

    @pl.when(i == 0)
    def _():
        own = jnp.logical_not(_lane_lo(LAT_T)).astype(jnp.int32) == n
        own_c = jnp.logical_not(_lane_lo(CTX_S)).astype(jnp.int32) == n
        kr = _rope(k_ref[...], cos_ref[...], sin_ref[...])
        kd_ref[0:SWA_W, :] = jnp.zeros((SWA_W, LANES), BF16)
        kd_ref[SWA_W + LAT_T:SWA_PAD, :] = jnp.zeros((SWA_W, LANES), BF16)
        kd_ref[SWA_W:SWA_W + LAT_T, :] = _double_head(kr, own).astype(BF16)
        kcd_ref[...] = _double_head(kc_ref[...], own_c).astype(BF16)
        v1_ref[0:SWA_W, :] = jnp.zeros((SWA_W, 2 * LANES), BF16)
        v1_ref[SWA_W + LAT_T:SWA_PAD, :] = jnp.zeros((SWA_W, 2 * LANES), BF16)
        v1_ref[SWA_W:SWA_W + LAT_T, :] = _with_ones(_double_head(v_ref[...], own))
        v1c_ref[...] = _with_ones(_double_head(vc_ref[...], own_c))

    lo = _lane_lo(SWA_BLK)
    start = pl.multiple_of(i * SWA_BLK, SWA_BLK)
    cos = cos_ref[pl.ds(start, SWA_BLK), :]
    sin = sin_ref[pl.ds(start, SWA_BLK), :]
    kw = kd_ref[pl.ds(start, SWA_KEYS), :]
    kcb = kcd_ref[...]
    a = lax.broadcasted_iota(jnp.int32, (2 * SWA_BLK, SWA_KEYS), 0) % SWA_BLK
    j = lax.broadcasted_iota(jnp.int32, (2 * SWA_BLK, SWA_KEYS), 1)
    kpos = start - SWA_W + j
    valid = (j - a >= 0) & (j - a <= 2 * SWA_W) & (kpos >= 0) & (kpos < LAT_T)
    sink_tab = sink_ref[...]
    scores = []
    for c in range(2):
        qq = _stack_halves(_rope(q_ref[:, c * LANES:(c + 1) * LANES], cos, sin) * Q_SCALE, lo)
        scores.append((jnp.where(valid, _qk(qq, kw), NEG), _qk(qq, kcb)))
    for c, (s_loc, s_ctx) in enumerate(scores):
        h = 4 * n + 2 * c
        sink = _pair_column(SWA_BLK, _pick_row(sink_tab, h), _pick_row(sink_tab, h + 1)) * LOG2E
        m = jnp.maximum(_row_max(s_loc, s_ctx), sink)
        e_loc = jnp.exp2(s_loc - m).astype(BF16)
        e_ctx = jnp.exp2(s_ctx - m).astype(BF16)
        r = _pv(e_loc, v1_ref[pl.ds(start, SWA_KEYS), :]) + _pv(e_ctx, v1c_ref[...])
        o = _wide_div(r, jnp.exp2(sink - m))
        o_ref[:, c * LANES:(c + 1) * LANES] = _unstack_halves(o, lo).astype(BF16)


def _swa(l, p, cache_swa, cos_tab, sin_tab, sink_tab):
    nq = LAT_T // SWA_BLK
    lat_blk = CTX_ROWS // LAT_T
    cache_spec = lambda which: pl.BlockSpec(
        (None, None, None, CTX_S, LANES), lambda b, n, i: (b, l, which, 0, 0))
    full = pl.BlockSpec((LAT_T, LANES), lambda b, n, i: (0, 0))
    return pl.pallas_call(
        _swa_kernel,
        grid=(N_LAT_B, 2, nq),
        in_specs=[
            pl.BlockSpec((SWA_BLK, 2 * LANES),
                         lambda b, n, i: (CTX_ROWS // SWA_BLK + b * nq + i, C_SQ // (2 * LANES) + n)),
            pl.BlockSpec((LAT_T, LANES), lambda b, n, i: (lat_blk + b, C_SK // LANES)),
            pl.BlockSpec((LAT_T, LANES), lambda b, n, i: (lat_blk + b, C_SV // LANES)),
            cache_spec(0),
            cache_spec(1),
            full,
            full,
            pl.BlockSpec((None, 8, LANES), lambda b, n, i: (l, 0, 0)),
        ],
        out_specs=pl.BlockSpec((SWA_BLK, 2 * LANES), lambda b, n, i: (b * nq + i, n)),
        out_shape=jax.ShapeDtypeStruct((LAT_ROWS, GW), BF16),
        scratch_shapes=[pltpu.VMEM((SWA_PAD, LANES), BF16), pltpu.VMEM((SWA_PAD, 2 * LANES), BF16),
                        pltpu.VMEM((CTX_S, LANES), BF16), pltpu.VMEM((CTX_S, 2 * LANES), BF16)],
        compiler_params=_cp(("parallel", "parallel", "arbitrary")),
        name="swa_latent",
    )(p, p, p, cache_swa, cache_swa, cos_tab, sin_tab, sink_tab)


DIFF_TQ = 512
DIFF_SPLIT = 2
DIFF_KEYS = LAT_T + CTX_S


def _diff_kernel(q_ref, k_ref, v_ref, kc_ref, vc_ref, cos_ref, sin_ref, lam_ref, dg_ref, o_ref,
                 kb_ref, vb_ref, *, lam_init):
    i = pl.program_id(2)

    @pl.when(i == 0)
    def _():
        kb_ref[0:LAT_T, :] = _rope(k_ref[...], cos_ref[...], sin_ref[...]).astype(BF16)
        kb_ref[LAT_T:DIFF_KEYS, :] = kc_ref[...].astype(BF16)
        vb_ref[0:LAT_T, :] = v_ref[...].astype(BF16)
        vb_ref[LAT_T:DIFF_KEYS, :] = vc_ref[...].astype(BF16)

    sub = DIFF_TQ // DIFF_SPLIT
    lo = _lane_lo(sub)
    lam = _lambda(lam_ref, lam_init)
    scores = []
    for t in range(DIFF_SPLIT):
        start = pl.multiple_of(i * DIFF_TQ + t * sub, sub)
        cos = cos_ref[pl.ds(start, sub), :]
        sin = sin_ref[pl.ds(start, sub), :]
        q = _rope(q_ref[t * sub:(t + 1) * sub, :], cos, sin) * Q_SCALE
        scores.append((_qk(jnp.where(lo, q, 0.0).astype(BF16), kb_ref[...]),
                       _qk(jnp.where(lo, 0.0, q).astype(BF16), kb_ref[...])))
    for t, (s1, s2) in enumerate(scores):
        o = _diff_finish(s1, s2, vb_ref[...], lam, dg_ref[...], lam_init)
        o_ref[t * sub:(t + 1) * sub, :] = o.astype(BF16)


def _diff_latent(l, p, cache_diff, cos_tab, sin_tab, diff_lambda, diff_g, lam_init):
    nq = LAT_T // DIFF_TQ
    lat_blk = CTX_ROWS // LAT_T
    cache_spec = lambda which: pl.BlockSpec(
        (None, None, None, CTX_S, LANES), lambda b, h, i: (b, l, which, 0, h))
    full = pl.BlockSpec((LAT_T, LANES), lambda b, h, i: (0, 0))
    return pl.pallas_call(
        functools.partial(_diff_kernel, lam_init=lam_init),
        grid=(N_LAT_B, GW // LANES, nq),
        in_specs=[
            pl.BlockSpec((DIFF_TQ, LANES),
                         lambda b, h, i: (CTX_ROWS // DIFF_TQ + b * nq + i, C_DQ // LANES + h)),
            pl.BlockSpec((LAT_T, LANES), lambda b, h, i: (lat_blk + b, C_DK // LANES + h)),
            pl.BlockSpec((LAT_T, LANES), lambda b, h, i: (lat_blk + b, C_DV // LANES + h)),
            cache_spec(0),
            cache_spec(1),
            full,
            full,
            pl.BlockSpec((None, 4, HD), lambda b, h, i: (l, 0, 0)),
            pl.BlockSpec((None, 1, LANES), lambda b, h, i: (l, 0, 0)),
        ],
        out_specs=pl.BlockSpec((DIFF_TQ, LANES), lambda b, h, i: (b * nq + i, h)),
        out_shape=jax.ShapeDtypeStruct((LAT_ROWS, GW), BF16),
        scratch_shapes=[pltpu.VMEM((DIFF_KEYS, LANES), BF16), pltpu.VMEM((DIFF_KEYS, LANES), BF16)],
        compiler_params=_cp(("parallel", "parallel", "arbitrary")),
        name="diff_latent",
    )(p, p, p, cache_diff, cache_diff, cos_tab, sin_tab, diff_lambda, diff_g)


OUTPROJ_SPLIT = 2


def _outproj_kernel(mc_ref, oa_ref, ob_ref, oc_ref, od_ref, x_ref, g_ref, m_ref, w_ref,
                    x1_ref, h2_ref, wb_ref, *, n_ctx_blocks):
    i = pl.program_id(0)
    tm = x_ref.shape[0]
    sub = tm // OUTPROJ_SPLIT

    @pl.when(i == 0)
    def _():
        wb_ref[...] = w_ref[...].astype(BF16)

    def finish(rows, y):
        x1 = x_ref[rows, :] + m_ref[2:3, :] * (_rms(y) * g_ref[1:2, :])
        x1_ref[rows, :] = x1
        h2 = (_rms(x1) * g_ref[2:3, :]) * (1.0 + m_ref[4:5, :]) + m_ref[3:4, :]
        h2_ref[rows, :] = h2.astype(BF16)

    @pl.when(i < n_ctx_blocks)
    def _():
        for s in range(OUTPROJ_SPLIT):
            rows = slice(s * sub, (s + 1) * sub)
            finish(rows, jnp.dot(mc_ref[rows, :], wb_ref[...], preferred_element_type=F32))

    @pl.when(i >= n_ctx_blocks)
    def _():
        for s in range(OUTPROJ_SPLIT):
            rows = slice(s * sub, (s + 1) * sub)
            mix = jnp.concatenate([oa_ref[rows, :], ob_ref[rows, :], oc_ref[rows, :], od_ref[rows, :]],
                                  axis=1)
            finish(rows, jnp.dot(mix, wb_ref[...], preferred_element_type=F32))


def _outproj(l, mix_ctx, lat_parts, x, norm_g, mods, w_out):
    tm = 512
    nc = CTX_ROWS // tm
    lat_spec = pl.BlockSpec((tm, GW), lambda i: (jnp.maximum(i - nc, 0), 0))
    return pl.pallas_call(
        functools.partial(_outproj_kernel, n_ctx_blocks=nc),
        grid=(ROWS // tm,),
        in_specs=[
            pl.BlockSpec((tm, D), lambda i: (jnp.minimum(i, nc - 1), 0)),
            lat_spec, lat_spec, lat_spec, lat_spec,
            pl.BlockSpec((tm, D), lambda i: (i, 0)),
            pl.BlockSpec((None, 4, D), lambda i: (l, 0, 0)),
            pl.BlockSpec((None, None, N_MOD, D), lambda i: (l, _row_group(i, tm), 0, 0)),
            pl.BlockSpec((None, D, D), lambda i: (l, 0, 0), pipeline_mode=pl.Buffered(1)),
        ],
        out_specs=[pl.BlockSpec((tm, D), lambda i: (i, 0)),
                   pl.BlockSpec((tm, D), lambda i: (i, 0))],
        out_shape=[jax.ShapeDtypeStruct((ROWS, D), F32), jax.ShapeDtypeStruct((ROWS, D), BF16)],
        scratch_shapes=[pltpu.VMEM((D, D), BF16)],
        compiler_params=_cp(("arbitrary",)),
        name="outproj",
    )(mix_ctx, *lat_parts, x, norm_g, mods, w_out)


MLP_TM, MLP_TH, MLP_TC = 1024, 512, 512
MLP_SPLIT = 2
MLP_NH = HIDDEN // MLP_TH
MLP_NC = D // MLP_TC


def _mlp_kernel(h_ref, x1_ref, g_ref, m_ref, w1_ref, w2_ref, o_ref, acc_ref, rs_ref):
    j = pl.program_id(1)
    sub = MLP_TM // MLP_SPLIT

    @pl.when(j == 0)
    def _():
        acc_ref[...] = jnp.zeros_like(acc_ref)

    @pl.when(j < MLP_NH)
    def _():
        w1b = w1_ref[...].astype(BF16)
        w2b = w2_ref[...].astype(BF16)
        for s in range(MLP_SPLIT):
            rows = slice(s * sub, (s + 1) * sub)
            a = jnp.dot(h_ref[rows, :], w1b, preferred_element_type=F32)
            a = jnp.square(jnp.maximum(a, 0.0)).astype(BF16)
            part = jnp.dot(a, w2b, preferred_element_type=F32)
            for c in range(MLP_NC):
                acc_ref[c, rows, :] += part[:, c * MLP_TC:(c + 1) * MLP_TC]

    @pl.when(j == MLP_NH)
    def _():
        ss = jnp.sum(jnp.square(acc_ref[0]), axis=-1, keepdims=True)
        for c in range(1, MLP_NC):
            ss += jnp.sum(jnp.square(acc_ref[c]), axis=-1, keepdims=True)
        rs_ref[...] = lax.rsqrt(ss * (1.0 / D) + EPS)

    @pl.when(j >= MLP_NH)
    def _():
        f = acc_ref[j - MLP_NH]
        o_ref[...] = x1_ref[...] + m_ref[5:6, :] * ((f * rs_ref[...]) * g_ref[3:4, :])


def _mlp(l, h2, x1, norm_g, mods, w1, w2):
    tm, th, tc = MLP_TM, MLP_TH, MLP_TC
    hid = lambda j: jnp.minimum(j, MLP_NH - 1)
    col = lambda j: jnp.maximum(j - MLP_NH, 0)
    return pl.pallas_call(
        _mlp_kernel,
        grid=(ROWS // tm, MLP_NH + MLP_NC),
        in_specs=[
            pl.BlockSpec((tm, D), lambda i, j: (i, 0)),
            pl.BlockSpec((tm, tc), lambda i, j: (i, col(j))),
            pl.BlockSpec((None, 4, tc), lambda i, j: (l, 0, col(j))),
            pl.BlockSpec((None, None, N_MOD, tc), lambda i, j: (l, _row_group(i, tm), 0, col(j))),
            pl.BlockSpec((None, D, th), lambda i, j: (l, 0, hid(j))),
            pl.BlockSpec((None, th, D), lambda i, j: (l, hid(j), 0)),
        ],
        out_specs=pl.BlockSpec((tm, tc), lambda i, j: (i, col(j))),
        out_shape=jax.ShapeDtypeStruct((ROWS, D), F32),
        scratch_shapes=[pltpu.VMEM((MLP_NC, tm, tc), F32), pltpu.VMEM((tm, 1), F32)],
        compiler_params=_cp(("parallel", "arbitrary")),
        name="mlp",
    )(h2, x1, norm_g, mods, w1, w2)


def _rope_tables():
    t = np.arange(LAT_T)
    rows = (t // GRID_W).astype(np.float32)
    cols = (t % GRID_W).astype(np.float32)
    n = HD // 4
    inv = jnp.asarray(10000.0, F32) ** (-jnp.arange(n, dtype=F32) / n)
    ar = jnp.asarray(rows)[:, None] * inv[None]
    ac = jnp.asarray(cols)[:, None] * inv[None]
    cos = jnp.concatenate([jnp.cos(ar), jnp.cos(ar), jnp.cos(ac), jnp.cos(ac)], axis=-1)
    sin = jnp.concatenate([-jnp.sin(ar), jnp.sin(ar), -jnp.sin(ac), jnp.sin(ac)], axis=-1)
    return jnp.tile(cos, (1, 2)), jnp.tile(sin, (1, 2))


def kernel(x_prompt, x_sample, cache_na_kv, cache_swa_kv, cache_diff_kv, c, c_ctx, ada_w, ada_b,
           norm_g, w_in, conv_w, na_rpb, swa_sink, diff_lambda, diff_norm_g, w_out, mlp_w1, mlp_w2):
    cvec = jnp.concatenate([c_ctx[None], c, jnp.zeros((8 - 1 - N_LAT_B, D), F32)], axis=0)
    mods = _modulation(cvec, ada_w, ada_b).reshape(DEPTH, 8, N_MOD, D)

    cache_na = cache_na_kv.reshape(N_LAT_B, DEPTH, 2, CTX_S, GW)
    cache_swa = cache_swa_kv.reshape(N_LAT_B, DEPTH, 2, CTX_S, LANES)
    cache_diff = cache_diff_kv.reshape(N_LAT_B, DEPTH, 2, CTX_S, GW)
    cos_tab, sin_tab = _rope_tables()
    sink_tab = jnp.broadcast_to(swa_sink[:, :, None], (DEPTH, 8, LANES))
    diff_g = diff_norm_g.reshape(DEPTH, 1, LANES)

    rpb_pad = jnp.pad(na_rpb, ((0, 0), (0, 0), (0, 0), (0, LANES - na_rpb.shape[-1])))

    x = jnp.concatenate([x_prompt.reshape(CTX_ROWS, D), x_sample.reshape(LAT_ROWS, D)], axis=0)
    kv = []
    for l in range(DEPTH):
        lam_init = 0.8 - 0.6 * math.exp(-0.3 * l)
        p = _inproj(l, x, norm_g, mods, w_in)
        mix_ctx, *kv = _ctx_mixer(l, p, conv_w, sink_tab, diff_lambda, diff_g, lam_init, kv)
        o_a = _na(l, p, cache_na, rpb_pad)
        o_b = _conv_latent(l, p, conv_w)
        o_c = _swa(l, p, cache_swa, cos_tab, sin_tab, sink_tab)
        o_d = _diff_latent(l, p, cache_diff, cos_tab, sin_tab, diff_lambda, diff_g, lam_init)
        x1, h2 = _outproj(l, mix_ctx, (o_a, o_b, o_c, o_d), x, norm_g, mods, w_out)
        x = _mlp(l, h2, x1, norm_g, mods, mlp_w1, mlp_w2)

    na_kv, swa_kv, diff_kv = kv
    return (x[:CTX_ROWS].reshape(N_CTX_B, CTX_S, D),
            x[CTX_ROWS:].reshape(N_LAT_B, LAT_T, D),
            na_kv.reshape(N_CTX_B, DEPTH, 2, CTX_S, 8, HD),
            swa_kv.reshape(N_CTX_B, DEPTH, 2, CTX_S, 2, HD),
            diff_kv.reshape(N_CTX_B, DEPTH, 2, CTX_S, 4, 2 * HD))
```

```python
import functools
import math

import numpy as np
import jax
import jax.numpy as jnp
from jax import lax
from jax.experimental import pallas as pl
from jax.experimental.pallas import tpu as pltpu

F32 = jnp.float32
BF16 = jnp.bfloat16

D = 2048
N_CTX_B, CTX_S = 16, 256
N_LAT_B, LAT_T = 2, 2048
CTX_ROWS = N_CTX_B * CTX_S
LAT_ROWS = N_LAT_B * LAT_T
ROWS = CTX_ROWS + LAT_ROWS
DEPTH = 4
GRID_W = 64
GRID_H = LAT_T // GRID_W
HD = 64
GW = 512
IN_COLS = 5376
HIDDEN = 4 * D
N_MOD = 6
EPS = 1e-6
NEG = -1e30
SCALE = HD ** -0.5
LOG2E = math.log2(math.e)
Q_SCALE = SCALE * LOG2E
NA_R, NA_C = 8, 16
SWA_W = 128
LANES = 128

C_NAQ, C_NAK, C_NAV = 0, 512, 1024
C_U, C_GB, C_GC = 1536, 2048, 2560
C_SQ, C_SK, C_SV = 3072, 3584, 3712
C_DQ, C_DK, C_DV = 3840, 4352, 4864

VMEM_LIMIT = 56 * 1024 * 1024


def _cp(sem):
    return pltpu.CompilerParams(dimension_semantics=sem, vmem_limit_bytes=VMEM_LIMIT)


def _rms(x):
    return x * lax.rsqrt(jnp.mean(x * x, axis=-1, keepdims=True) + EPS)


def _qk(q, k):
    return lax.dot_general(q, k, (((1,), (1,)), ((), ())), preferred_element_type=F32)


def _pv(p, v):
    return jnp.dot(p, v, preferred_element_type=F32)


def _lane_lo(rows):
    return lax.broadcasted_iota(jnp.int32, (rows, LANES), 1) < HD


def _row_group(i, tm):
    return jnp.maximum(i * tm - LAT_T, 0) // LAT_T


def _mod_kernel(c_ref, w_ref, b_ref, o_ref):
    cv = c_ref[...]
    s = cv * (1.0 / (1.0 + jnp.exp(-cv)))
    o_ref[...] = jnp.dot(s.astype(BF16), w_ref[...].astype(BF16),
                         preferred_element_type=F32) + b_ref[...]


def _modulation(cvec, ada_w, ada_b):
    tn = 1024
    n = N_MOD * D
    return pl.pallas_call(
        _mod_kernel,
        grid=(DEPTH, n // tn),
        in_specs=[
            pl.BlockSpec((8, D), lambda l, j: (0, 0)),
            pl.BlockSpec((None, D, tn), lambda l, j: (l, 0, j)),
            pl.BlockSpec((None, 1, tn), lambda l, j: (l, 0, j)),
        ],
        out_specs=pl.BlockSpec((None, 8, tn), lambda l, j: (l, 0, j)),
        out_shape=jax.ShapeDtypeStruct((DEPTH, 8, n), F32),
        compiler_params=_cp(("parallel", "parallel")),
        name="modulation",
    )(cvec, ada_w, ada_b.reshape(DEPTH, 1, n))


INPROJ_SPLIT = 4


INPROJ_PAIR = 2


def _inproj_kernel(x_ref, g_ref, m_ref, w_ref, o_ref, h_ref, wb_ref):
    j = pl.program_id(1)
    r = pl.program_id(2)

    @pl.when(r == 0)
    def _():
        wb_ref[...] = w_ref[...].astype(BF16)

    @pl.when(j == 0)
    def _():
        sub = x_ref.shape[0] // INPROJ_SPLIT
        for s in range(INPROJ_SPLIT):
            rows = slice(s * sub, (s + 1) * sub)
            h = _rms(x_ref[rows, :]) * g_ref[0:1, :]
            h = (h * (1.0 + m_ref[1:2, :]) + m_ref[0:1, :]).astype(BF16)
            h_ref[r, rows, :] = h
            o_ref[rows, :] = jnp.dot(h, wb_ref[...], preferred_element_type=F32)

    @pl.when(j > 0)
    def _():
        o_ref[...] = jnp.dot(h_ref[r], wb_ref[...], preferred_element_type=F32)


def _inproj(l, x, norm_g, mods, w_in):
    tm, tn = 1024, 768
    blk = lambda ip, r: INPROJ_PAIR * ip + r
    x_blk = lambda ip, j, r: jnp.where(j == 0, blk(ip, r), blk(ip, INPROJ_PAIR - 1))
    return pl.pallas_call(
        _inproj_kernel,
        grid=(ROWS // (INPROJ_PAIR * tm), IN_COLS // tn, INPROJ_PAIR),
        in_specs=[
            pl.BlockSpec((tm, D), lambda ip, j, r: (x_blk(ip, j, r), 0)),
            pl.BlockSpec((None, 4, D), lambda ip, j, r: (l, 0, 0)),
            pl.BlockSpec((None, None, N_MOD, D),
                         lambda ip, j, r: (l, _row_group(blk(ip, r), tm), 0, 0)),
            pl.BlockSpec((None, D, tn), lambda ip, j, r: (l, 0, j)),
        ],
        out_specs=pl.BlockSpec((tm, tn), lambda ip, j, r: (blk(ip, r), j)),
        out_shape=jax.ShapeDtypeStruct((ROWS, IN_COLS), F32),
        scratch_shapes=[pltpu.VMEM((INPROJ_PAIR, tm, D), BF16), pltpu.VMEM((D, tn), BF16)],
        compiler_params=_cp(("arbitrary", "arbitrary", "arbitrary")),
        name="inproj",
    )(x, norm_g, mods, w_in)


def _conv3(u, gb, gc, w):
    z = gc * u
    n = z.shape[0]
    row = lax.broadcasted_iota(jnp.int32, z.shape, 0)
    zp = jnp.where(row == 0, 0.0, pltpu.roll(z, 1, 0))
    zn = jnp.where(row == n - 1, 0.0, pltpu.roll(z, n - 1, 0))
    return gb * (zp * w[0:1, :] + z * w[1:2, :] + zn * w[2:3, :])


def _lambda(lam_ref, lam_init):
    lp = lam_ref[...]
    a = jnp.sum(lp[0:1, :] * lp[1:2, :], axis=-1, keepdims=True)
    b = jnp.sum(lp[2:3, :] * lp[3:4, :], axis=-1, keepdims=True)
    return jnp.exp(a) - jnp.exp(b) + lam_init


def _stack_halves(q, lo):
    return jnp.concatenate([jnp.where(lo, q, 0.0), jnp.where(lo, 0.0, q)], axis=0).astype(BF16)


def _unstack_halves(o, lo):
    n = o.shape[0] // 2
    return jnp.where(lo, o[0:n], o[n:2 * n])


def _with_ones(v):
    return jnp.concatenate([v, jnp.ones_like(v)], axis=1).astype(BF16)


def _row_max(*parts):
    m = jnp.max(parts[0], axis=-1, keepdims=True)
    for p in parts[1:]:
        m = jnp.maximum(m, jnp.max(p, axis=-1, keepdims=True))
    return m


def _wide_div(r, extra=None):
    den = r[:, LANES:2 * LANES]
    return r[:, 0:LANES] / (den if extra is None else den + extra)


def _pair_column(n, first, second):
    row = lax.broadcasted_iota(jnp.int32, (2 * n, 1), 0)
    return jnp.where(row < n, first, second)


def _diff_finish(s1, s2, vb, lam, dg, lam_init):
    e1 = jnp.exp2(s1 - jnp.max(s1, axis=-1, keepdims=True))
    e2 = jnp.exp2(s2 - jnp.max(s2, axis=-1, keepdims=True))
    sum1 = jnp.sum(e1, axis=-1, keepdims=True)
    c = lam * sum1 / jnp.sum(e2, axis=-1, keepdims=True)
    o = _pv((e1 - c * e2).astype(BF16), vb) / sum1
    return (_rms(o) * dg) * (1.0 - lam_init)


def _double_head(x, own):
    return jnp.where(own, x, pltpu.roll(x, HD, 1))


def _pick_row(tab, idx):
    row = lax.broadcasted_iota(jnp.int32, tab.shape, 0)
    return jnp.sum(jnp.where(row == idx, tab, 0.0), axis=0, keepdims=True)[:, 0:1]


def _swap16(x):
    lane = lax.broadcasted_iota(jnp.int32, x.shape, 1)
    return jnp.where(lane % 32 < 16, pltpu.roll(x, LANES - 16, 1), pltpu.roll(x, 16, 1))


def _rope(x, cos, sin):
    return x * cos + _swap16(x) * sin


def _ctx_kernel(p_ref, cw_ref, sink_ref, lam_ref, dg_ref, *rest, lam_init, first_layer):
    o_ref, na_ref, swa_ref, df_ref = rest[-4:]
    n = CTX_S
    lo = _lane_lo(n)
    lane_hi = jnp.logical_not(lo)

    def blk(c):
        return p_ref[:, c:c + LANES]

    kv_parts = ((na_ref, C_NAK, C_NAV, GW), (swa_ref, C_SK, C_SV, LANES), (df_ref, C_DK, C_DV, GW))
    for ref, ck, cv, w in kv_parts:
        if first_layer:
            ref[0, 0] = p_ref[:, ck:ck + w]
            ref[0, 1] = p_ref[:, cv:cv + w]
            ref[1:DEPTH] = jnp.zeros((DEPTH - 1, 2, n, w), F32)
        else:
            ref[0] = p_ref[:, ck:ck + w]
            ref[1] = p_ref[:, cv:cv + w]

    k2 = blk(C_SK)
    v2 = blk(C_SV)
    kd = [_double_head(k2, lo if kvh == 0 else lane_hi).astype(BF16) for kvh in range(2)]
    s_mha = [_qk(_stack_halves(blk(C_NAQ + hp * LANES) * Q_SCALE, lo),
                 blk(C_NAK + hp * LANES).astype(BF16)) for hp in range(GW // LANES)]
    s_gqa = [_qk(_stack_halves(blk(C_SQ + g * LANES) * Q_SCALE, lo), kd[g // 2])
             for g in range(GW // LANES)]
    s_dif = [_qk(_stack_halves(blk(C_DQ + h * LANES) * Q_SCALE, lo),
                 blk(C_DK + h * LANES).astype(BF16)) for h in range(GW // LANES)]

    for hp, s in enumerate(s_mha):
        e = jnp.exp2(s - _row_max(s)).astype(BF16)
        r = _pv(e, _with_ones(blk(C_NAV + hp * LANES)))
        o_ref[:, hp * LANES:(hp + 1) * LANES] = _unstack_halves(_wide_div(r), lo).astype(BF16)

    for cb in range(GW // LANES):
        c = cb * LANES
        y = _conv3(blk(C_U + c), blk(C_GB + c), blk(C_GC + c), cw_ref[:, c:c + LANES])
        o_ref[:, GW + c:GW + c + LANES] = y.astype(BF16)

    sink_tab = sink_ref[...]
    v1d = [_with_ones(_double_head(v2, lo if kvh == 0 else lane_hi)) for kvh in range(2)]
    for g, s in enumerate(s_gqa):
        sink = _pair_column(n, _pick_row(sink_tab, 2 * g), _pick_row(sink_tab, 2 * g + 1)) * LOG2E
        m = jnp.maximum(_row_max(s), sink)
        e = jnp.exp2(s - m).astype(BF16)
        r = _pv(e, v1d[g // 2])
        col = 2 * GW + g * LANES
        o_ref[:, col:col + LANES] = _unstack_halves(_wide_div(r, jnp.exp2(sink - m)), lo).astype(BF16)

    lam = _lambda(lam_ref, lam_init)
    dg = dg_ref[...]
    for h, s in enumerate(s_dif):
        o = _diff_finish(s[0:n], s[n:2 * n], blk(C_DV + h * LANES).astype(BF16), lam, dg, lam_init)
        col = 3 * GW + h * LANES
        o_ref[:, col:col + LANES] = o.astype(BF16)


def _ctx_mixer(l, p, conv_w, sink_tab, diff_lambda, diff_g, lam_init, kv_prev):
    first = not kv_prev
    kv_shapes = [(N_CTX_B, DEPTH, 2, CTX_S, w) for w in (GW, LANES, GW)]
    if first:
        kv_spec = lambda w: pl.BlockSpec((None, DEPTH, 2, CTX_S, w), lambda b: (b, 0, 0, 0, 0))
    else:
        kv_spec = lambda w: pl.BlockSpec((None, None, 2, CTX_S, w), lambda b: (b, l, 0, 0, 0))
    n_in = 5
    aliases = {n_in + k: 1 + k for k in range(len(kv_prev))}
    return pl.pallas_call(
        functools.partial(_ctx_kernel, lam_init=lam_init, first_layer=first),
        grid=(N_CTX_B,),
        in_specs=[
            pl.BlockSpec((CTX_S, IN_COLS), lambda b: (b, 0)),
            pl.BlockSpec((None, 3, GW), lambda b: (l, 0, 0)),
            pl.BlockSpec((None, 8, LANES), lambda b: (l, 0, 0)),
            pl.BlockSpec((None, 4, HD), lambda b: (l, 0, 0)),
            pl.BlockSpec((None, 1, LANES), lambda b: (l, 0, 0)),
        ] + [pl.BlockSpec(memory_space=pl.ANY)] * len(kv_prev),
        out_specs=[pl.BlockSpec((CTX_S, D), lambda b: (b, 0)), kv_spec(GW), kv_spec(LANES), kv_spec(GW)],
        out_shape=[jax.ShapeDtypeStruct((CTX_ROWS, D), BF16)]
        + [jax.ShapeDtypeStruct(s, F32) for s in kv_shapes],
        input_output_aliases=aliases,
        compiler_params=_cp(("parallel",)),
        name="ctx_mixer",
    )(p, conv_w, sink_tab, diff_lambda, diff_g, *kv_prev)


NA_GROUP = 4
NA_KEYS = NA_R * GRID_W


def _na_build_bias(rpb_ref, bt_ref):
    n_dr = 2 * NA_R - 1
    for hh in range(2):
        rows = slice(hh * GRID_W, (hh + 1) * GRID_W)
        for dr in range(n_dr):
            row = jnp.broadcast_to(rpb_ref[hh, dr:dr + 1, :] * LOG2E, (GRID_W, LANES))
            t_lo = pltpu.roll(row, LANES - (NA_C - 1), 1, stride=1, stride_axis=0)
            t_hi = pltpu.roll(row, HD - (NA_C - 1), 1, stride=1, stride_axis=0)
            for j in range(NA_R):
                d0 = dr - j
                if 0 <= d0 < NA_R:
                    c = (j // 2) * LANES
                    if j % 2 == 0:
                        bt_ref[d0, rows, c:c + HD] = t_lo[:, 0:HD]
                    else:
                        bt_ref[d0, rows, c + HD:c + LANES] = t_hi[:, HD:LANES]


def _na_kernel(q_ref, k_ref, v_ref, kc_ref, vc_ref, rpb_ref, o_ref, kb_ref, v1_ref, bt_ref):
    _na_build_bias(rpb_ref, bt_ref)
    kb_ref[...] = k_ref[...].astype(BF16)
    v1_ref[...] = _with_ones(v_ref[...])
    kcb = kc_ref[...].astype(BF16)
    v1c = _with_ones(vc_ref[...])
    lo = _lane_lo(GRID_W)
    qc = lax.broadcasted_iota(jnp.int32, (2 * GRID_W, NA_KEYS), 0) % GRID_W
    kc = lax.broadcasted_iota(jnp.int32, (2 * GRID_W, NA_KEYS), 1) % GRID_W
    cs = jnp.clip(qc - NA_C // 2, 0, GRID_W - NA_C)
    col_ok = (kc >= cs) & (kc < cs + NA_C)

    def group(g, carry):
        scores = []
        for a in range(NA_GROUP):
            r = g * NA_GROUP + a
            rs = jnp.clip(r - NA_R // 2, 0, GRID_H - NA_R)
            d0 = rs - r + NA_R - 1
            q = q_ref[pl.ds(pl.multiple_of(r * GRID_W, GRID_W), GRID_W), :] * Q_SCALE
            qq = _stack_halves(q, lo)
            kw = kb_ref[pl.ds(pl.multiple_of(rs * GRID_W, GRID_W), NA_KEYS), :]
            s_loc = jnp.where(col_ok, _qk(qq, kw) + bt_ref[d0], NEG)
            scores.append((r, rs, s_loc, _qk(qq, kcb)))
        for r, rs, s_loc, s_ctx in scores:
            m = _row_max(s_loc, s_ctx)
            e_loc = jnp.exp2(s_loc - m).astype(BF16)
            e_ctx = jnp.exp2(s_ctx - m).astype(BF16)
            v1w = v1_ref[pl.ds(pl.multiple_of(rs * GRID_W, GRID_W), NA_KEYS), :]
            o = _wide_div(_pv(e_loc, v1w) + _pv(e_ctx, v1c))
            o_ref[pl.ds(pl.multiple_of(r * GRID_W, GRID_W), GRID_W), :] = (
                _unstack_halves(o, lo).astype(BF16))
        return carry

    lax.fori_loop(0, GRID_H // NA_GROUP, group, 0)


def _na(l, p, cache_na, rpb_pad):
    lat_blk = CTX_ROWS // LAT_T
    cache_spec = lambda which: pl.BlockSpec(
        (None, None, None, CTX_S, LANES), lambda b, hp: (b, l, which, 0, hp))
    return pl.pallas_call(
        _na_kernel,
        grid=(N_LAT_B, GW // LANES),
        in_specs=[
            pl.BlockSpec((LAT_T, LANES), lambda b, hp: (lat_blk + b, C_NAQ // LANES + hp)),
            pl.BlockSpec((LAT_T, LANES), lambda b, hp: (lat_blk + b, C_NAK // LANES + hp)),
            pl.BlockSpec((LAT_T, LANES), lambda b, hp: (lat_blk + b, C_NAV // LANES + hp)),
            cache_spec(0),
            cache_spec(1),
            pl.BlockSpec((None, 2, 2 * NA_R - 1, LANES), lambda b, hp: (l, hp, 0, 0)),
        ],
        out_specs=pl.BlockSpec((LAT_T, LANES), lambda b, hp: (b, hp)),
        out_shape=jax.ShapeDtypeStruct((LAT_ROWS, GW), BF16),
        scratch_shapes=[pltpu.VMEM((LAT_T, LANES), BF16), pltpu.VMEM((LAT_T, 2 * LANES), BF16),
                        pltpu.VMEM((NA_R, 2 * GRID_W, NA_KEYS), F32)],
        compiler_params=_cp(("parallel", "parallel")),
        name="na_latent",
    )(p, p, p, cache_na, cache_na, rpb_pad)


def _conv_kernel(u_ref, gb_ref, gc_ref, w_ref, o_ref):
    o_ref[...] = _conv3(u_ref[...], gb_ref[...], gc_ref[...], w_ref[...]).astype(BF16)


def _conv_latent(l, p, conv_w):
    lat_blk = CTX_ROWS // LAT_T
    spec = lambda c0: pl.BlockSpec((LAT_T, LANES), lambda b, c: (lat_blk + b, c0 // LANES + c))
    return pl.pallas_call(
        _conv_kernel,
        grid=(N_LAT_B, GW // LANES),
        in_specs=[spec(C_U), spec(C_GB), spec(C_GC),
                  pl.BlockSpec((None, 3, LANES), lambda b, c: (l, 0, c))],
        out_specs=pl.BlockSpec((LAT_T, LANES), lambda b, c: (b, c)),
        out_shape=jax.ShapeDtypeStruct((LAT_ROWS, GW), BF16),
        compiler_params=_cp(("parallel", "parallel")),
        name="conv_latent",
    )(p, p, p, conv_w)


SWA_BLK = 256
SWA_KEYS = SWA_BLK + 2 * SWA_W
SWA_PAD = LAT_T + 2 * SWA_W


def _swa_kernel(q_ref, k_ref, v_ref, kc_ref, vc_ref, cos_ref, sin_ref, sink_ref, o_ref,
                kd_ref, v1_ref, kcd_ref, v1c_ref):
    n = pl.program_id(1)
    i = pl.program_id(2)

    @pl.when(i == 0)
    def _():
        own = jnp.logical_not(_lane_lo(LAT_T)).astype(jnp.int32) == n
        own_c = jnp.logical_not(_lane_lo(CTX_S)).astype(jnp.int32) == n
        kr = _rope(k_ref[...], cos_ref[...], sin_ref[...])
        kd_ref[0:SWA_W, :] = jnp.zeros((SWA_W, LANES), BF16)
        kd_ref[SWA_W + LAT_T:SWA_PAD, :] = jnp.zeros((SWA_W, LANES), BF16)
        kd_ref[SWA_W:SWA_W + LAT_T, :] = _double_head(kr, own).astype(BF16)
        kcd_ref[...] = _double_head(kc_ref[...], own_c).astype(BF16)
        v1_ref[0:SWA_W, :] = jnp.zeros((SWA_W, 2 * LANES), BF16)
        v1_ref[SWA_W + LAT_T:SWA_PAD, :] = jnp.zeros((SWA_W, 2 * LANES), BF16)
        v1_ref[SWA_W:SWA_W + LAT_T, :] = _with_ones(_double_head(v_ref[...], own))
        v1c_ref[...] = _with_ones(_double_head(vc_ref[...], own_c))

    lo = _lane_lo(SWA_BLK)
    start = pl.multiple_of(i * SWA_BLK, SWA_BLK)
    cos = cos_ref[pl.ds(start, SWA_BLK), :]
    sin = sin_ref[pl.ds(start, SWA_BLK), :]
    kw = kd_ref[pl.ds(start, SWA_KEYS), :]
    kcb = kcd_ref[...]
    a = lax.broadcasted_iota(jnp.int32, (2 * SWA_BLK, SWA_KEYS), 0) % SWA_BLK
    j = lax.broadcasted_iota(jnp.int32, (2 * SWA_BLK, SWA_KEYS), 1)
    kpos = start - SWA_W + j
    valid = (j - a >= 0) & (j - a <= 2 * SWA_W) & (kpos >= 0) & (kpos < LAT_T)
    sink_tab = sink_ref[...]
    scores = []
    for c in range(2):
        qq = _stack_halves(_rope(q_ref[:, c * LANES:(c + 1) * LANES], cos, sin) * Q_SCALE, lo)
        scores.append((jnp.where(valid, _qk(qq, kw), NEG), _qk(qq, kcb)))
    for c, (s_loc, s_ctx) in enumerate(scores):
        h = 4 * n + 2 * c
        sink = _pair_column(SWA_BLK, _pick_row(sink_tab, h), _pick_row(sink_tab, h + 1)) * LOG2E
        m = jnp.maximum(_row_max(s_loc, s_ctx), sink)
        e_loc = jnp.exp2(s_loc - m).astype(BF16)
        e_ctx = jnp.exp2(s_ctx - m).astype(BF16)
        r = _pv(e_loc, v1_ref[pl.ds(start, SWA_KEYS), :]) + _pv(e_ctx, v1c_ref[...])
        o = _wide_div(r, jnp.exp2(sink - m))
        o_ref[:, c * LANES:(c + 1) * LANES] = _unstack_halves(o, lo).astype(BF16)


def _swa(l, p, cache_swa, cos_tab, sin_tab, sink_tab):
    nq = LAT_T // SWA_BLK
    lat_blk = CTX_ROWS // LAT_T
    cache_spec = lambda which: pl.BlockSpec(
        (None, None, None, CTX_S, LANES), lambda b, n, i: (b, l, which, 0, 0))
    full = pl.BlockSpec((LAT_T, LANES), lambda b, n, i: (0, 0))
    return pl.pallas_call(
        _swa_kernel,
        grid=(N_LAT_B, 2, nq),
        in_specs=[
            pl.BlockSpec((SWA_BLK, 2 * LANES),
                         lambda b, n, i: (CTX_ROWS // SWA_BLK + b * nq + i, C_SQ // (2 * LANES) + n)),
            pl.BlockSpec((LAT_T, LANES), lambda b, n, i: (lat_blk + b, C_SK // LANES)),
            pl.BlockSpec((LAT_T, LANES), lambda b, n, i: (lat_blk + b, C_SV // LANES)),
            cache_spec(0),
            cache_spec(1),
            full,
            full,
            pl.BlockSpec((None, 8, LANES), lambda b, n, i: (l, 0, 0)),
        ],
        out_specs=pl.BlockSpec((SWA_BLK, 2 * LANES), lambda b, n, i: (b * nq + i, n)),
        out_shape=jax.ShapeDtypeStruct((LAT_ROWS, GW), BF16),
        scratch_shapes=[pltpu.VMEM((SWA_PAD, LANES), BF16), pltpu.VMEM((SWA_PAD, 2 * LANES), BF16),
                        pltpu.VMEM((CTX_S, LANES), BF16), pltpu.VMEM((CTX_S, 2 * LANES), BF16)],
        compiler_params=_cp(("parallel", "parallel", "arbitrary")),
        name="swa_latent",
    )(p, p, p, cache_swa, cache_swa, cos_tab, sin_tab, sink_tab)


DIFF_TQ = 512
DIFF_SPLIT = 2
DIFF_KEYS = LAT_T + CTX_S


def _diff_kernel(q_ref, k_ref, v_ref, kc_ref, vc_ref, cos_ref, sin_ref, lam_ref, dg_ref, o_ref,
                 kb_ref, vb_ref, *, lam_init):
    i = pl.program_id(2)

    @pl.when(i == 0)
    def _():
        kb_ref[0:LAT_T, :] = _rope(k_ref[...], cos_ref[...], sin_ref[...]).astype(BF16)
        kb_ref[LAT_T:DIFF_KEYS, :] = kc_ref[...].astype(BF16)
        vb_ref[0:LAT_T, :] = v_ref[...].astype(BF16)
        vb_ref[LAT_T:DIFF_KEYS, :] = vc_ref[...].astype(BF16)

    sub = DIFF_TQ // DIFF_SPLIT
    lo = _lane_lo(sub)
    lam = _lambda(lam_ref, lam_init)
    scores = []
    for t in range(DIFF_SPLIT):
        start = pl.multiple_of(i * DIFF_TQ + t * sub, sub)
        cos = cos_ref[pl.ds(start, sub), :]
        sin = sin_ref[pl.ds(start, sub), :]
        q = _rope(q_ref[t * sub:(t + 1) * sub, :], cos, sin) * Q_SCALE
        scores.append((_qk(jnp.where(lo, q, 0.0).astype(BF16), kb_ref[...]),
                       _qk(jnp.where(lo, 0.0, q).astype(BF16), kb_ref[...])))
    for t, (s1, s2) in enumerate(scores):
        o = _diff_finish(s1, s2, vb_ref[...], lam, dg_ref[...], lam_init)
        o_ref[t * sub:(t + 1) * sub, :] = o.astype(BF16)


def _diff_latent(l, p, cache_diff, cos_tab, sin_tab, diff_lambda, diff_g, lam_init):
    nq = LAT_T // DIFF_TQ
    lat_blk = CTX_ROWS // LAT_T
    cache_spec = lambda which: pl.BlockSpec(
        (None, None, None, CTX_S, LANES), lambda b, h, i: (b, l, which, 0, h))
    full = pl.BlockSpec((LAT_T, LANES), lambda b, h, i: (0, 0))
    return pl.pallas_call(
        functools.partial(_diff_kernel, lam_init=lam_init),
        grid=(N_LAT_B, GW // LANES, nq),
        in_specs=[
            pl.BlockSpec((DIFF_TQ, LANES),
                         lambda b, h, i: (CTX_ROWS // DIFF_TQ + b * nq + i, C_DQ // LANES + h)),
            pl.BlockSpec((LAT_T, LANES), lambda b, h, i: (lat_blk + b, C_DK // LANES + h)),
            pl.BlockSpec((LAT_T, LANES), lambda b, h, i: (lat_blk + b, C_DV // LANES + h)),
            cache_spec(0),
            cache_spec(1),
            full,
            full,
            pl.BlockSpec((None, 4, HD), lambda b, h, i: (l, 0, 0)),
            pl.BlockSpec((None, 1, LANES), lambda b, h, i: (l, 0, 0)),
        ],
        out_specs=pl.BlockSpec((DIFF_TQ, LANES), lambda b, h, i: (b * nq + i, h)),
        out_shape=jax.ShapeDtypeStruct((LAT_ROWS, GW), BF16),
        scratch_shapes=[pltpu.VMEM((DIFF_KEYS, LANES), BF16), pltpu.VMEM((DIFF_KEYS, LANES), BF16)],
        compiler_params=_cp(("parallel", "parallel", "arbitrary")),
        name="diff_latent",
    )(p, p, p, cache_diff, cache_diff, cos_tab, sin_tab, diff_lambda, diff_g)


OUTPROJ_SPLIT = 2


def _outproj_kernel(mc_ref, oa_ref, ob_ref, oc_ref, od_ref, x_ref, g_ref, m_ref, w_ref,
                    x1_ref, h2_ref, wb_ref, *, n_ctx_blocks):
    i = pl.program_id(0)
    tm = x_ref.shape[0]
    sub = tm // OUTPROJ_SPLIT

    @pl.when(i == 0)
    def _():
        wb_ref[...] = w_ref[...].astype(BF16)

    def finish(rows, y):
        x1 = x_ref[rows, :] + m_ref[2:3, :] * (_rms(y) * g_ref[1:2, :])
        x1_ref[rows, :] = x1
        h2 = (_rms(x1) * g_ref[2:3, :]) * (1.0 + m_ref[4:5, :]) + m_ref[3:4, :]
        h2_ref[rows, :] = h2.astype(BF16)

    @pl.when(i < n_ctx_blocks)
    def _():
        for s in range(OUTPROJ_SPLIT):
            rows = slice(s * sub, (s + 1) * sub)
            finish(rows, jnp.dot(mc_ref[rows, :], wb_ref[...], preferred_element_type=F32))

    @pl.when(i >= n_ctx_blocks)
    def _():
        for s in range(OUTPROJ_SPLIT):
            rows = slice(s * sub, (s + 1) * sub)
            mix = jnp.concatenate([oa_ref[rows, :], ob_ref[rows, :], oc_ref[rows, :], od_ref[rows, :]],
                                  axis=1)
            finish(rows, jnp.dot(mix, wb_ref[...], preferred_element_type=F32))


def _outproj(l, mix_ctx, lat_parts, x, norm_g, mods, w_out):
    tm = 512
    nc = CTX_ROWS // tm
    lat_spec = pl.BlockSpec((tm, GW), lambda i: (jnp.maximum(i - nc, 0), 0))
    return pl.pallas_call(
        functools.partial(_outproj_kernel, n_ctx_blocks=nc),
        grid=(ROWS // tm,),
        in_specs=[
            pl.BlockSpec((tm, D), lambda i: (jnp.minimum(i, nc - 1), 0)),
            lat_spec, lat_spec, lat_spec, lat_spec,
            pl.BlockSpec((tm, D), lambda i: (i, 0)),
            pl.BlockSpec((None, 4, D), lambda i: (l, 0, 0)),
            pl.BlockSpec((None, None, N_MOD, D), lambda i: (l, _row_group(i, tm), 0, 0)),
            pl.BlockSpec((None, D, D), lambda i: (l, 0, 0), pipeline_mode=pl.Buffered(1)),
        ],
        out_specs=[pl.BlockSpec((tm, D), lambda i: (i, 0)),
                   pl.BlockSpec((tm, D), lambda i: (i, 0))],
        out_shape=[jax.ShapeDtypeStruct((ROWS, D), F32), jax.ShapeDtypeStruct((ROWS, D), BF16)],
        scratch_shapes=[pltpu.VMEM((D, D), BF16)],
        compiler_params=_cp(("arbitrary",)),
        name="outproj",
    )(mix_ctx, *lat_parts, x, norm_g, mods, w_out)


MLP_TM, MLP_TH, MLP_TC = 1024, 512, 512
MLP_SPLIT = 2
MLP_NH = HIDDEN // MLP_TH
MLP_NC = D // MLP_TC


def _mlp_kernel(h_ref, x1_ref, g_ref, m_ref, w1_ref, w2_ref, o_ref, acc_ref, rs_ref):
    j = pl.program_id(1)
    sub = MLP_TM // MLP_SPLIT

    def hidden_chunk(first):
        w1b = w1_ref[...].astype(BF16)
        w2b = w2_ref[...].astype(BF16)
        for s in range(MLP_SPLIT):
            rows = slice(s * sub, (s + 1) * sub)
            a = jnp.dot(h_ref[rows, :], w1b, preferred_element_type=F32)
            a = jnp.square(jnp.maximum(a, 0.0)).astype(BF16)
            part = jnp.dot(a, w2b, preferred_element_type=F32)
            for c in range(MLP_NC):
                chunk = part[:, c * MLP_TC:(c + 1) * MLP_TC]
                if first:
                    acc_ref[c, rows, :] = chunk
                else:
                    acc_ref[c, rows, :] += chunk

    pl.when(j == 0)(functools.partial(hidden_chunk, True))
    pl.when((j > 0) & (j < MLP_NH))(functools.partial(hidden_chunk, False))

    @pl.when(j == MLP_NH)
    def _():
        ss = jnp.sum(jnp.square(acc_ref[0]), axis=-1, keepdims=True)
        for c in range(1, MLP_NC):
            ss += jnp.sum(jnp.square(acc_ref[c]), axis=-1, keepdims=True)
        rs_ref[...] = lax.rsqrt(ss * (1.0 / D) + EPS)

    @pl.when(j >= MLP_NH)
    def _():
        f = acc_ref[j - MLP_NH]
        o_ref[...] = x1_ref[...] + m_ref[5:6, :] * ((f * rs_ref[...]) * g_ref[3:4, :])


def _mlp(l, h2, x1, norm_g, mods, w1, w2):
    tm, th, tc = MLP_TM, MLP_TH, MLP_TC
    hid = lambda j: jnp.minimum(j, MLP_NH - 1)
    col = lambda j: jnp.maximum(j - MLP_NH, 0)
    return pl.pallas_call(
        _mlp_kernel,
        grid=(ROWS // tm, MLP_NH + MLP_NC),
        in_specs=[
            pl.BlockSpec((tm, D), lambda i, j: (i, 0)),
            pl.BlockSpec((tm, tc), lambda i, j: (i, col(j))),
            pl.BlockSpec((None, 4, tc), lambda i, j: (l, 0, col(j))),
            pl.BlockSpec((None, None, N_MOD, tc), lambda i, j: (l, _row_group(i, tm), 0, col(j))),
            pl.BlockSpec((None, D, th), lambda i, j: (l, 0, hid(j))),
            pl.BlockSpec((None, th, D), lambda i, j: (l, hid(j), 0)),
        ],
        out_specs=pl.BlockSpec((tm, tc), lambda i, j: (i, col(j))),
        out_shape=jax.ShapeDtypeStruct((ROWS, D), F32),
        scratch_shapes=[pltpu.VMEM((MLP_NC, tm, tc), F32), pltpu.VMEM((tm, 1), F32)],
        compiler_params=_cp(("parallel", "arbitrary")),
        name="mlp",
    )(h2, x1, norm_g, mods, w1, w2)


def _rope_tables():
    t = np.arange(LAT_T)
    rows = (t // GRID_W).astype(np.float32)
    cols = (t % GRID_W).astype(np.float32)
    n = HD // 4
    inv = jnp.asarray(10000.0, F32) ** (-jnp.arange(n, dtype=F32) / n)
    ar = jnp.asarray(rows)[:, None] * inv[None]
    ac = jnp.asarray(cols)[:, None] * inv[None]
    cos = jnp.concatenate([jnp.cos(ar), jnp.cos(ar), jnp.cos(ac), jnp.cos(ac)], axis=-1)
    sin = jnp.concatenate([-jnp.sin(ar), jnp.sin(ar), -jnp.sin(ac), jnp.sin(ac)], axis=-1)
    return jnp.tile(cos, (1, 2)), jnp.tile(sin, (1, 2))


def kernel(x_prompt, x_sample, cache_na_kv, cache_swa_kv, cache_diff_kv, c, c_ctx, ada_w, ada_b,
           norm_g, w_in, conv_w, na_rpb, swa_sink, diff_lambda, diff_norm_g, w_out, mlp_w1, mlp_w2):
    cvec = jnp.concatenate([c_ctx[None], c, jnp.zeros((8 - 1 - N_LAT_B, D), F32)], axis=0)
    mods = _modulation(cvec, ada_w, ada_b).reshape(DEPTH, 8, N_MOD, D)

    cache_na = cache_na_kv.reshape(N_LAT_B, DEPTH, 2, CTX_S, GW)
    cache_swa = cache_swa_kv.reshape(N_LAT_B, DEPTH, 2, CTX_S, LANES)
    cache_diff = cache_diff_kv.reshape(N_LAT_B, DEPTH, 2, CTX_S, GW)
    cos_tab, sin_tab = _rope_tables()
    sink_tab = jnp.broadcast_to(swa_sink[:, :, None], (DEPTH, 8, LANES))
    diff_g = diff_norm_g.reshape(DEPTH, 1, LANES)

    rpb_pad = jnp.pad(na_rpb, ((0, 0), (0, 0), (0, 0), (0, LANES - na_rpb.shape[-1])))

    x = jnp.concatenate([x_prompt.reshape(CTX_ROWS, D), x_sample.reshape(LAT_ROWS, D)], axis=0)
    kv = []
    for l in range(DEPTH):
        lam_init = 0.8 - 0.6 * math.exp(-0.3 * l)
        p = _inproj(l, x, norm_g, mods, w_in)
        mix_ctx, *kv = _ctx_mixer(l, p, conv_w, sink_tab, diff_lambda, diff_g, lam_init, kv)
        o_a = _na(l, p, cache_na, rpb_pad)
        o_b = _conv_latent(l, p, conv_w)
        o_c = _swa(l, p, cache_swa, cos_tab, sin_tab, sink_tab)
        o_d = _diff_latent(l, p, cache_diff, cos_tab, sin_tab, diff_lambda, diff_g, lam_init)
        x1, h2 = _outproj(l, mix_ctx, (o_a, o_b, o_c, o_d), x, norm_g, mods, w_out)
        x = _mlp(l, h2, x1, norm_g, mods, mlp_w1, mlp_w2)

    na_kv, swa_kv, diff_kv = kv
    return (x[:CTX_ROWS].reshape(N_CTX_B, CTX_S, D),
            x[CTX_ROWS:].reshape(N_LAT_B, LAT_T, D),
            na_kv.reshape(N_CTX_B, DEPTH, 2, CTX_S, 8, HD),
            swa_kv.reshape(N_CTX_B, DEPTH, 2, CTX_S, 2, HD),
            diff_kv.reshape(N_CTX_B, DEPTH, 2, CTX_S, 4, 2 * HD))
```

```python
import functools
import math

import numpy as np
import jax
import jax.numpy as jnp
from jax import lax
from jax.experimental import pallas as pl
from jax.experimental.pallas import tpu as pltpu

F32 = jnp.float32
BF16 = jnp.bfloat16

D = 2048
N_CTX_B, CTX_S = 16, 256
N_LAT_B, LAT_T = 2, 2048
CTX_ROWS = N_CTX_B * CTX_S
LAT_ROWS = N_LAT_B * LAT_T
ROWS = CTX_ROWS + LAT_ROWS
DEPTH = 4
GRID_W = 64
GRID_H = LAT_T // GRID_W
HD = 64
GW = 512
IN_COLS = 5376
HIDDEN = 4 * D
N_MOD = 6
EPS = 1e-6
NEG = -1e30
SCALE = HD ** -0.5
LOG2E = math.log2(math.e)
Q_SCALE = SCALE * LOG2E
NA_R, NA_C = 8, 16
SWA_W = 128
LANES = 128

C_NAQ, C_NAK, C_NAV = 0, 512, 1024
C_U, C_GB, C_GC = 1536, 2048, 2560
C_SQ, C_SK, C_SV = 3072, 3584, 3712
C_DQ, C_DK, C_DV = 3840, 4352, 4864

VMEM_LIMIT = 56 * 1024 * 1024


def _cp(sem):
    return pltpu.CompilerParams(dimension_semantics=sem, vmem_limit_bytes=VMEM_LIMIT)


def _rms(x):
    return x * lax.rsqrt(jnp.mean(x * x, axis=-1, keepdims=True) + EPS)


def _qk(q, k):
    return lax.dot_general(q, k, (((1,), (1,)), ((), ())), preferred_element_type=F32)


def _pv(p, v):
    return jnp.dot(p, v, preferred_element_type=F32)


def _lane_lo(rows):
    return lax.broadcasted_iota(jnp.int32, (rows, LANES), 1) < HD


def _row_group(i, tm):
    return jnp.maximum(i * tm - LAT_T, 0) // LAT_T


def _mod_kernel(c_ref, w_ref, b_ref, o_ref):
    cv = c_ref[...]
    s = cv * (1.0 / (1.0 + jnp.exp(-cv)))
    o_ref[...] = jnp.dot(s.astype(BF16), w_ref[...].astype(BF16),
                         preferred_element_type=F32) + b_ref[...]


def _modulation(cvec, ada_w, ada_b):
    tn = 1024
    n = N_MOD * D
    return pl.pallas_call(
        _mod_kernel,
        grid=(DEPTH, n // tn),
        in_specs=[
            pl.BlockSpec((8, D), lambda l, j: (0, 0)),
            pl.BlockSpec((None, D, tn), lambda l, j: (l, 0, j)),
            pl.BlockSpec((None, 1, tn), lambda l, j: (l, 0, j)),
        ],
        out_specs=pl.BlockSpec((None, 8, tn), lambda l, j: (l, 0, j)),
        out_shape=jax.ShapeDtypeStruct((DEPTH, 8, n), F32),
        compiler_params=_cp(("parallel", "parallel")),
        name="modulation",
    )(cvec, ada_w, ada_b.reshape(DEPTH, 1, n))


INPROJ_SPLIT = 4


INPROJ_PAIR = 2


def _inproj_kernel(x_ref, g_ref, m_ref, w_ref, o_ref, h_ref, wb_ref):
    j = pl.program_id(1)
    r = pl.program_id(2)

    @pl.when(r == 0)
    def _():
        wb_ref[...] = w_ref[...].astype(BF16)

    @pl.when(j == 0)
    def _():
        sub = x_ref.shape[0] // INPROJ_SPLIT
        for s in range(INPROJ_SPLIT):
            rows = slice(s * sub, (s + 1) * sub)
            h = _rms(x_ref[rows, :]) * g_ref[0:1, :]
            h = (h * (1.0 + m_ref[1:2, :]) + m_ref[0:1, :]).astype(BF16)
            h_ref[r, rows, :] = h
            o_ref[rows, :] = jnp.dot(h, wb_ref[...], preferred_element_type=F32)

    @pl.when(j > 0)
    def _():
        o_ref[...] = jnp.dot(h_ref[r], wb_ref[...], preferred_element_type=F32)


def _inproj(l, x, norm_g, mods, w_in):
    tm, tn = 1024, 768
    blk = lambda ip, r: INPROJ_PAIR * ip + r
    x_blk = lambda ip, j, r: jnp.where(j == 0, blk(ip, r), blk(ip, INPROJ_PAIR - 1))
    return pl.pallas_call(
        _inproj_kernel,
        grid=(ROWS // (INPROJ_PAIR * tm), IN_COLS // tn, INPROJ_PAIR),
        in_specs=[
            pl.BlockSpec((tm, D), lambda ip, j, r: (x_blk(ip, j, r), 0)),
            pl.BlockSpec((None, 4, D), lambda ip, j, r: (l, 0, 0)),
            pl.BlockSpec((None, None, N_MOD, D),
                         lambda ip, j, r: (l, _row_group(blk(ip, r), tm), 0, 0)),
            pl.BlockSpec((None, D, tn), lambda ip, j, r: (l, 0, j)),
        ],
        out_specs=pl.BlockSpec((tm, tn), lambda ip, j, r: (blk(ip, r), j)),
        out_shape=jax.ShapeDtypeStruct((ROWS, IN_COLS), F32),
        scratch_shapes=[pltpu.VMEM((INPROJ_PAIR, tm, D), BF16), pltpu.VMEM((D, tn), BF16)],
        compiler_params=_cp(("arbitrary", "arbitrary", "arbitrary")),
        name="inproj",
    )(x, norm_g, mods, w_in)


def _conv3(u, gb, gc, w):
    z = gc * u
    n = z.shape[0]
    row = lax.broadcasted_iota(jnp.int32, z.shape, 0)
    zp = jnp.where(row == 0, 0.0, pltpu.roll(z, 1, 0))
    zn = jnp.where(row == n - 1, 0.0, pltpu.roll(z, n - 1, 0))
    return gb * (zp * w[0:1, :] + z * w[1:2, :] + zn * w[2:3, :])


def _lambda(lam_ref, lam_init):
    lp = lam_ref[...]
    a = jnp.sum(lp[0:1, :] * lp[1:2, :], axis=-1, keepdims=True)
    b = jnp.sum(lp[2:3, :] * lp[3:4, :], axis=-1, keepdims=True)
    return jnp.exp(a) - jnp.exp(b) + lam_init


def _stack_halves(q, lo):
    return jnp.concatenate([jnp.where(lo, q, 0.0), jnp.where(lo, 0.0, q)], axis=0).astype(BF16)


def _unstack_halves(o, lo):
    n = o.shape[0] // 2
    return jnp.where(lo, o[0:n], o[n:2 * n])


def _with_ones(v):
    return jnp.concatenate([v, jnp.ones_like(v)], axis=1).astype(BF16)


def _row_max(*parts):
    m = jnp.max(parts[0], axis=-1, keepdims=True)
    for p in parts[1:]:
        m = jnp.maximum(m, jnp.max(p, axis=-1, keepdims=True))
    return m


def _wide_div(r, extra=None):
    den = r[:, LANES:2 * LANES]
    return r[:, 0:LANES] / (den if extra is None else den + extra)


def _pair_column(n, first, second):
    row = lax.broadcasted_iota(jnp.int32, (2 * n, 1), 0)
    return jnp.where(row < n, first, second)


def _diff_finish(s1, s2, vb, lam, dg, lam_init):
    e1 = jnp.exp2(s1 - jnp.max(s1, axis=-1, keepdims=True))
    e2 = jnp.exp2(s2 - jnp.max(s2, axis=-1, keepdims=True))
    sum1 = jnp.sum(e1, axis=-1, keepdims=True)
    c = lam * sum1 / jnp.sum(e2, axis=-1, keepdims=True)
    o = _pv((e1 - c * e2).astype(BF16), vb) / sum1
    return (_rms(o) * dg) * (1.0 - lam_init)


def _double_head(x, own):
    return jnp.where(own, x, pltpu.roll(x, HD, 1))


def _pick_row(tab, idx):
    row = lax.broadcasted_iota(jnp.int32, tab.shape, 0)
    return jnp.sum(jnp.where(row == idx, tab, 0.0), axis=0, keepdims=True)[:, 0:1]


def _swap16(x):
    lane = lax.broadcasted_iota(jnp.int32, x.shape, 1)
    return jnp.where(lane % 32 < 16, pltpu.roll(x, LANES - 16, 1), pltpu.roll(x, 16, 1))


def _rope(x, cos, sin):
    return x * cos + _swap16(x) * sin


def _ctx_kernel(p_ref, cw_ref, sink_ref, lam_ref, dg_ref, *rest, lam_init, first_layer):
    o_ref, na_ref, swa_ref, df_ref = rest[-4:]
    n = CTX_S
    lo = _lane_lo(n)
    lane_hi = jnp.logical_not(lo)

    def blk(c):
        return p_ref[:, c:c + LANES]

    kv_parts = ((na_ref, C_NAK, C_NAV, GW), (swa_ref, C_SK, C_SV, LANES), (df_ref, C_DK, C_DV, GW))
    for ref, ck, cv, w in kv_parts:
        if first_layer:
            ref[0, 0] = p_ref[:, ck:ck + w]
            ref[0, 1] = p_ref[:, cv:cv + w]
            ref[1:DEPTH] = jnp.zeros((DEPTH - 1, 2, n, w), F32)
        else:
            ref[0] = p_ref[:, ck:ck + w]
            ref[1] = p_ref[:, cv:cv + w]

    k2 = blk(C_SK)
    v2 = blk(C_SV)
    kd = [_double_head(k2, lo if kvh == 0 else lane_hi).astype(BF16) for kvh in range(2)]
    s_mha = [_qk(_stack_halves(blk(C_NAQ + hp * LANES) * Q_SCALE, lo),
                 blk(C_NAK + hp * LANES).astype(BF16)) for hp in range(GW // LANES)]
    s_gqa = [_qk(_stack_halves(blk(C_SQ + g * LANES) * Q_SCALE, lo), kd[g // 2])
             for g in range(GW // LANES)]
    s_dif = [_qk(_stack_halves(blk(C_DQ + h * LANES) * Q_SCALE, lo),
                 blk(C_DK + h * LANES).astype(BF16)) for h in range(GW // LANES)]

    for hp, s in enumerate(s_mha):
        e = jnp.exp2(s - _row_max(s)).astype(BF16)
        r = _pv(e, _with_ones(blk(C_NAV + hp * LANES)))
        o_ref[:, hp * LANES:(hp + 1) * LANES] = _unstack_halves(_wide_div(r), lo).astype(BF16)

    for cb in range(GW // LANES):
        c = cb * LANES
        y = _conv3(blk(C_U + c), blk(C_GB + c), blk(C_GC + c), cw_ref[:, c:c + LANES])
        o_ref[:, GW + c:GW + c + LANES] = y.astype(BF16)

    sink_tab = sink_ref[...]
    v1d = [_with_ones(_double_head(v2, lo if kvh == 0 else lane_hi)) for kvh in range(2)]
    for g, s in enumerate(s_gqa):
        sink = _pair_column(n, _pick_row(sink_tab, 2 * g), _pick_row(sink_tab, 2 * g + 1)) * LOG2E
        m = jnp.maximum(_row_max(s), sink)
        e = jnp.exp2(s - m).astype(BF16)
        r = _pv(e, v1d[g // 2])
        col = 2 * GW + g * LANES
        o_ref[:, col:col + LANES] = _unstack_halves(_wide_div(r, jnp.exp2(sink - m)), lo).astype(BF16)

    lam = _lambda(lam_ref, lam_init)
    dg = dg_ref[...]
    for h, s in enumerate(s_dif):
        o = _diff_finish(s[0:n], s[n:2 * n], blk(C_DV + h * LANES).astype(BF16), lam, dg, lam_init)
        col = 3 * GW + h * LANES
        o_ref[:, col:col + LANES] = o.astype(BF16)


def _ctx_mixer(l, p, conv_w, sink_tab, diff_lambda, diff_g, lam_init, kv_prev):
    first = not kv_prev
    kv_shapes = [(N_CTX_B, DEPTH, 2, CTX_S, w) for w in (GW, LANES, GW)]
    if first:
        kv_spec = lambda w: pl.BlockSpec((None, DEPTH, 2, CTX_S, w), lambda b: (b, 0, 0, 0, 0))
    else:
        kv_spec = lambda w: pl.BlockSpec((None, None, 2, CTX_S, w), lambda b: (b, l, 0, 0, 0))
    n_in = 5
    aliases = {n_in + k: 1 + k for k in range(len(kv_prev))}
    return pl.pallas_call(
        functools.partial(_ctx_kernel, lam_init=lam_init, first_layer=first),
        grid=(N_CTX_B,),
        in_specs=[
            pl.BlockSpec((CTX_S, IN_COLS), lambda b: (b, 0)),
            pl.BlockSpec((None, 3, GW), lambda b: (l, 0, 0)),
            pl.BlockSpec((None, 8, LANES), lambda b: (l, 0, 0)),
            pl.BlockSpec((None, 4, HD), lambda b: (l, 0, 0)),
            pl.BlockSpec((None, 1, LANES), lambda b: (l, 0, 0)),
        ] + [pl.BlockSpec(memory_space=pl.ANY)] * len(kv_prev),
        out_specs=[pl.BlockSpec((CTX_S, D), lambda b: (b, 0)), kv_spec(GW), kv_spec(LANES), kv_spec(GW)],
        out_shape=[jax.ShapeDtypeStruct((CTX_ROWS, D), BF16)]
        + [jax.ShapeDtypeStruct(s, F32) for s in kv_shapes],
        input_output_aliases=aliases,
        compiler_params=_cp(("parallel",)),
        name="ctx_mixer",
    )(p, conv_w, sink_tab, diff_lambda, diff_g, *kv_prev)


NA_GROUP = 4
NA_KEYS = NA_R * GRID_W


def _na_build_bias(rpb_ref, bt_ref):
    n_dr = 2 * NA_R - 1
    for hh in range(2):
        rows = slice(hh * GRID_W, (hh + 1) * GRID_W)
        for dr in range(n_dr):
            row = jnp.broadcast_to(rpb_ref[hh, dr:dr + 1, :] * LOG2E, (GRID_W, LANES))
            t_lo = pltpu.roll(row, LANES - (NA_C - 1), 1, stride=1, stride_axis=0)
            t_hi = pltpu.roll(row, HD - (NA_C - 1), 1, stride=1, stride_axis=0)
            for j in range(NA_R):
                d0 = dr - j
                if 0 <= d0 < NA_R:
                    c = (j // 2) * LANES
                    if j % 2 == 0:
                        bt_ref[d0, rows, c:c + HD] = t_lo[:, 0:HD]
                    else:
                        bt_ref[d0, rows, c + HD:c + LANES] = t_hi[:, HD:LANES]


def _na_kernel(q_ref, k_ref, v_ref, kc_ref, vc_ref, rpb_ref, o_ref, kb_ref, v1_ref, bt_ref):
    _na_build_bias(rpb_ref, bt_ref)
    kb_ref[...] = k_ref[...].astype(BF16)
    v1_ref[...] = _with_ones(v_ref[...])
    kcb = kc_ref[...].astype(BF16)
    v1c = _with_ones(vc_ref[...])
    lo = _lane_lo(GRID_W)
    qc = lax.broadcasted_iota(jnp.int32, (2 * GRID_W, NA_KEYS), 0) % GRID_W
    kc = lax.broadcasted_iota(jnp.int32, (2 * GRID_W, NA_KEYS), 1) % GRID_W
    cs = jnp.clip(qc - NA_C // 2, 0, GRID_W - NA_C)
    col_ok = (kc >= cs) & (kc < cs + NA_C)

    def group(g, carry):
        scores = []
        for a in range(NA_GROUP):
            r = g * NA_GROUP + a
            rs = jnp.clip(r - NA_R // 2, 0, GRID_H - NA_R)
            d0 = rs - r + NA_R - 1
            q = q_ref[pl.ds(pl.multiple_of(r * GRID_W, GRID_W), GRID_W), :] * Q_SCALE
            qq = _stack_halves(q, lo)
            kw = kb_ref[pl.ds(pl.multiple_of(rs * GRID_W, GRID_W), NA_KEYS), :]
            s_loc = jnp.where(col_ok, _qk(qq, kw) + bt_ref[d0], NEG)
            scores.append((r, rs, s_loc, _qk(qq, kcb)))
        for r, rs, s_loc, s_ctx in scores:
            m = _row_max(s_loc, s_ctx)
            e_loc = jnp.exp2(s_loc - m).astype(BF16)
            e_ctx = jnp.exp2(s_ctx - m).astype(BF16)
            v1w = v1_ref[pl.ds(pl.multiple_of(rs * GRID_W, GRID_W), NA_KEYS), :]
            o = _wide_div(_pv(e_loc, v1w) + _pv(e_ctx, v1c))
            o_ref[pl.ds(pl.multiple_of(r * GRID_W, GRID_W), GRID_W), :] = (
                _unstack_halves(o, lo).astype(BF16))
        return carry

    lax.fori_loop(0, GRID_H // NA_GROUP, group, 0)


def _na(l, p, cache_na, rpb_pad):
    lat_blk = CTX_ROWS // LAT_T
    cache_spec = lambda which: pl.BlockSpec(
        (None, None, None, CTX_S, LANES), lambda b, hp: (b, l, which, 0, hp))
    return pl.pallas_call(
        _na_kernel,
        grid=(N_LAT_B, GW // LANES),
        in_specs=[
            pl.BlockSpec((LAT_T, LANES), lambda b, hp: (lat_blk + b, C_NAQ // LANES + hp)),
            pl.BlockSpec((LAT_T, LANES), lambda b, hp: (lat_blk + b, C_NAK // LANES + hp)),
            pl.BlockSpec((LAT_T, LANES), lambda b, hp: (lat_blk + b, C_NAV // LANES + hp)),
            cache_spec(0),
            cache_spec(1),
            pl.BlockSpec((None, 2, 2 * NA_R - 1, LANES), lambda b, hp: (l, hp, 0, 0)),
        ],
        out_specs=pl.BlockSpec((LAT_T, LANES), lambda b, hp: (b, hp)),
        out_shape=jax.ShapeDtypeStruct((LAT_ROWS, GW), BF16),
        scratch_shapes=[pltpu.VMEM((LAT_T, LANES), BF16), pltpu.VMEM((LAT_T, 2 * LANES), BF16),
                        pltpu.VMEM((NA_R, 2 * GRID_W, NA_KEYS), F32)],
        compiler_params=_cp(("parallel", "parallel")),
        name="na_latent",
    )(p, p, p, cache_na, cache_na, rpb_pad)


def _conv_kernel(u_ref, gb_ref, gc_ref, w_ref, o_ref):
    o_ref[...] = _conv3(u_ref[...], gb_ref[...], gc_ref[...], w_ref[...]).astype(BF16)


def _conv_latent(l, p, conv_w):
    lat_blk = CTX_ROWS // LAT_T
    spec = lambda c0: pl.BlockSpec((LAT_T, LANES), lambda b, c: (lat_blk + b, c0 // LANES + c))
    return pl.pallas_call(
        _conv_kernel,
        grid=(N_LAT_B, GW // LANES),
        in_specs=[spec(C_U), spec(C_GB), spec(C_GC),
                  pl.BlockSpec((None, 3, LANES), lambda b, c: (l, 0, c))],
        out_specs=pl.BlockSpec((LAT_T, LANES), lambda b, c: (b, c)),
        out_shape=jax.ShapeDtypeStruct((LAT_ROWS, GW), BF16),
        compiler_params=_cp(("parallel", "parallel")),
        name="conv_latent",
    )(p, p, p, conv_w)


SWA_BLK = 256
SWA_KEYS = SWA_BLK + 2 * SWA_W
SWA_PAD = LAT_T + 2 * SWA_W


def _swa_kernel(q_ref, k_ref, v_ref, kc_ref, vc_ref, cos_ref, sin_ref, sink_ref, o_ref,
                kd_ref, v1_ref, kcd_ref, v1c_ref):
    n = pl.program_id(1)
    i = pl.program_id(2)

    @pl.when(i == 0)
    def _():
        own = jnp.logical_not(_lane_lo(LAT_T)).astype(jnp.int32) == n
        own_c = jnp.logical_not(_lane_lo(CTX_S)).astype(jnp.int32) == n
        kr = _rope(k_ref[...], cos_ref[...], sin_ref[...])
        kd_ref[0:SWA_W, :] = jnp.zeros((SWA_W, LANES), BF16)
        kd_ref[SWA_W + LAT_T:SWA_PAD, :] = jnp.zeros((SWA_W, LANES), BF16)
        kd_ref[SWA_W:SWA_W + LAT_T, :] = _double_head(kr, own).astype(BF16)
        kcd_ref[...] = _double_head(kc_ref[...], own_c).astype(BF16)
        v1_ref[0:SWA_W, :] = jnp.zeros((SWA_W, 2 * LANES), BF16)
        v1_ref[SWA_W + LAT_T:SWA_PAD, :] = jnp.zeros((SWA_W, 2 * LANES), BF16)
        v1_ref[SWA_W:SWA_W + LAT_T, :] = _with_ones(_double_head(v_ref[...], own))
        v1c_ref[...] = _with_ones(_double_head(vc_ref[...], own_c))

    lo = _lane_lo(SWA_BLK)
    start = pl.multiple_of(i * SWA_BLK, SWA_BLK)
    cos = cos_ref[pl.ds(start, SWA_BLK), :]
    sin = sin_ref[pl.ds(start, SWA_BLK), :]
    kw = kd_ref[pl.ds(start, SWA_KEYS), :]
    kcb = kcd_ref[...]
    a = lax.broadcasted_iota(jnp.int32, (2 * SWA_BLK, SWA_KEYS), 0) % SWA_BLK
    j = lax.broadcasted_iota(jnp.int32, (2 * SWA_BLK, SWA_KEYS), 1)
    kpos = start - SWA_W + j
    valid = (j - a >= 0) & (j - a <= 2 * SWA_W) & (kpos >= 0) & (kpos < LAT_T)
    sink_tab = sink_ref[...]
    scores = []
    for c in range(2):
        qq = _stack_halves(_rope(q_ref[:, c * LANES:(c + 1) * LANES], cos, sin) * Q_SCALE, lo)
        scores.append((jnp.where(valid, _qk(qq, kw), NEG), _qk(qq, kcb)))
    for c, (s_loc, s_ctx) in enumerate(scores):
        h = 4 * n + 2 * c
        sink = _pair_column(SWA_BLK, _pick_row(sink_tab, h), _pick_row(sink_tab, h + 1)) * LOG2E
        m = jnp.maximum(_row_max(s_loc, s_ctx), sink)
        e_loc = jnp.exp2(s_loc - m).astype(BF16)
        e_ctx = jnp.exp2(s_ctx - m).astype(BF16)
        r = _pv(e_loc, v1_ref[pl.ds(start, SWA_KEYS), :]) + _pv(e_ctx, v1c_ref[...])
        o = _wide_div(r, jnp.exp2(sink - m))
        o_ref[:, c * LANES:(c + 1) * LANES] = _unstack_halves(o, lo).astype(BF16)


def _swa(l, p, cache_swa, cos_tab, sin_tab, sink_tab):
    nq = LAT_T // SWA_BLK
    lat_blk = CTX_ROWS // LAT_T
    cache_spec = lambda which: pl.BlockSpec(
        (None, None, None, CTX_S, LANES), lambda b, n, i: (b, l, which, 0, 0))
    full = pl.BlockSpec((LAT_T, LANES), lambda b, n, i: (0, 0))
    return pl.pallas_call(
        _swa_kernel,
        grid=(N_LAT_B, 2, nq),
        in_specs=[
            pl.BlockSpec((SWA_BLK, 2 * LANES),
                         lambda b, n, i: (CTX_ROWS // SWA_BLK + b * nq + i, C_SQ // (2 * LANES) + n)),
            pl.BlockSpec((LAT_T, LANES), lambda b, n, i: (lat_blk + b, C_SK // LANES)),
            pl.BlockSpec((LAT_T, LANES), lambda b, n, i: (lat_blk + b, C_SV // LANES)),
            cache_spec(0),
            cache_spec(1),
            full,
            full,
            pl.BlockSpec((None, 8, LANES), lambda b, n, i: (l, 0, 0)),
        ],
        out_specs=pl.BlockSpec((SWA_BLK, 2 * LANES), lambda b, n, i: (b * nq + i, n)),
        out_shape=jax.ShapeDtypeStruct((LAT_ROWS, GW), BF16),
        scratch_shapes=[pltpu.VMEM((SWA_PAD, LANES), BF16), pltpu.VMEM((SWA_PAD, 2 * LANES), BF16),
                        pltpu.VMEM((CTX_S, LANES), BF16), pltpu.VMEM((CTX_S, 2 * LANES), BF16)],
        compiler_params=_cp(("parallel", "parallel", "arbitrary")),
        name="swa_latent",
    )(p, p, p, cache_swa, cache_swa, cos_tab, sin_tab, sink_tab)


DIFF_TQ = 1024
DIFF_SPLIT = 4
DIFF_KEYS = LAT_T + CTX_S


def _diff_kernel(q_ref, k_ref, v_ref, kc_ref, vc_ref, cos_ref, sin_ref, lam_ref, dg_ref, o_ref,
                 kb_ref, vb_ref, *, lam_init):
    i = pl.program_id(2)

    @pl.when(i == 0)
    def _():
        kb_ref[0:LAT_T, :] = _rope(k_ref[...], cos_ref[...], sin_ref[...]).astype(BF16)
        kb_ref[LAT_T:DIFF_KEYS, :] = kc_ref[...].astype(BF16)
        vb_ref[0:LAT_T, :] = v_ref[...].astype(BF16)
        vb_ref[LAT_T:DIFF_KEYS, :] = vc_ref[...].astype(BF16)

    sub = DIFF_TQ // DIFF_SPLIT
    lo = _lane_lo(sub)
    lam = _lambda(lam_ref, lam_init)
    scores = []
    for t in range(DIFF_SPLIT):
        start = pl.multiple_of(i * DIFF_TQ + t * sub, sub)
        cos = cos_ref[pl.ds(start, sub), :]
        sin = sin_ref[pl.ds(start, sub), :]
        q = _rope(q_ref[t * sub:(t + 1) * sub, :], cos, sin) * Q_SCALE
        scores.append((_qk(jnp.where(lo, q, 0.0).astype(BF16), kb_ref[...]),
                       _qk(jnp.where(lo, 0.0, q).astype(BF16), kb_ref[...])))
    for t, (s1, s2) in enumerate(scores):
        o = _diff_finish(s1, s2, vb_ref[...], lam, dg_ref[...], lam_init)
        o_ref[t * sub:(t + 1) * sub, :] = o.astype(BF16)


def _diff_latent(l, p, cache_diff, cos_tab, sin_tab, diff_lambda, diff_g, lam_init):
    nq = LAT_T // DIFF_TQ
    lat_blk = CTX_ROWS // LAT_T
    cache_spec = lambda which: pl.BlockSpec(
        (None, None, None, CTX_S, LANES), lambda b, h, i: (b, l, which, 0, h))
    full = pl.BlockSpec((LAT_T, LANES), lambda b, h, i: (0, 0))
    return pl.pallas_call(
        functools.partial(_diff_kernel, lam_init=lam_init),
        grid=(N_LAT_B, GW // LANES, nq),
        in_specs=[
            pl.BlockSpec((DIFF_TQ, LANES),
                         lambda b, h, i: (CTX_ROWS // DIFF_TQ + b * nq + i, C_DQ // LANES + h)),
            pl.BlockSpec((LAT_T, LANES), lambda b, h, i: (lat_blk + b, C_DK // LANES + h)),
            pl.BlockSpec((LAT_T, LANES), lambda b, h, i: (lat_blk + b, C_DV // LANES + h)),
            cache_spec(0),
            cache_spec(1),
            full,
            full,
            pl.BlockSpec((None, 4, HD), lambda b, h, i: (l, 0, 0)),
            pl.BlockSpec((None, 1, LANES), lambda b, h, i: (l, 0, 0)),
        ],
        out_specs=pl.BlockSpec((DIFF_TQ, LANES), lambda b, h, i: (b * nq + i, h)),
        out_shape=jax.ShapeDtypeStruct((LAT_ROWS, GW), BF16),
        scratch_shapes=[pltpu.VMEM((DIFF_KEYS, LANES), BF16), pltpu.VMEM((DIFF_KEYS, LANES), BF16)],
        compiler_params=_cp(("parallel", "parallel", "arbitrary")),
        name="diff_latent",
    )(p, p, p, cache_diff, cache_diff, cos_tab, sin_tab, diff_lambda, diff_g)


OUTPROJ_SPLIT = 2


def _outproj_kernel(mc_ref, oa_ref, ob_ref, oc_ref, od_ref, x_ref, g_ref, m_ref, w_ref,
                    x1_ref, h2_ref, wb_ref, *, n_ctx_blocks):
    i = pl.program_id(0)
    tm = x_ref.shape[0]
    sub = tm // OUTPROJ_SPLIT

    @pl.when(i == 0)
    def _():
        wb_ref[...] = w_ref[...].astype(BF16)

    def finish(rows, y):
        x1 = x_ref[rows, :] + m_ref[2:3, :] * (_rms(y) * g_ref[1:2, :])
        x1_ref[rows, :] = x1
        h2 = (_rms(x1) * g_ref[2:3, :]) * (1.0 + m_ref[4:5, :]) + m_ref[3:4, :]
        h2_ref[rows, :] = h2.astype(BF16)

    @pl.when(i < n_ctx_blocks)
    def _():
        for s in range(OUTPROJ_SPLIT):
            rows = slice(s * sub, (s + 1) * sub)
            finish(rows, jnp.dot(mc_ref[rows, :], wb_ref[...], preferred_element_type=F32))

    @pl.when(i >= n_ctx_blocks)
    def _():
        for s in range(OUTPROJ_SPLIT):
            rows = slice(s * sub, (s + 1) * sub)
            mix = jnp.concatenate([oa_ref[rows, :], ob_ref[rows, :], oc_ref[rows, :], od_ref[rows, :]],
                                  axis=1)
            finish(rows, jnp.dot(mix, wb_ref[...], preferred_element_type=F32))


def _outproj(l, mix_ctx, lat_parts, x, norm_g, mods, w_out):
    tm = 512
    nc = CTX_ROWS // tm
    lat_spec = pl.BlockSpec((tm, GW), lambda i: (jnp.maximum(i - nc, 0), 0))
    return pl.pallas_call(
        functools.partial(_outproj_kernel, n_ctx_blocks=nc),
        grid=(ROWS // tm,),
        in_specs=[
            pl.BlockSpec((tm, D), lambda i: (jnp.minimum(i, nc - 1), 0)),
            lat_spec, lat_spec, lat_spec, lat_spec,
            pl.BlockSpec((tm, D), lambda i: (i, 0)),
            pl.BlockSpec((None, 4, D), lambda i: (l, 0, 0)),
            pl.BlockSpec((None, None, N_MOD, D), lambda i: (l, _row_group(i, tm), 0, 0)),
            pl.BlockSpec((None, D, D), lambda i: (l, 0, 0), pipeline_mode=pl.Buffered(1)),
        ],
        out_specs=[pl.BlockSpec((tm, D), lambda i: (i, 0)),
                   pl.BlockSpec((tm, D), lambda i: (i, 0))],
        out_shape=[jax.ShapeDtypeStruct((ROWS, D), F32), jax.ShapeDtypeStruct((ROWS, D), BF16)],
        scratch_shapes=[pltpu.VMEM((D, D), BF16)],
        compiler_params=_cp(("arbitrary",)),
        name="outproj",
    )(mix_ctx, *lat_parts, x, norm_g, mods, w_out)


MLP_TM, MLP_TH, MLP_TC = 1024, 512, 512
MLP_SPLIT = 2
MLP_NH = HIDDEN // MLP_TH
MLP_NC = D // MLP_TC


def _mlp_kernel(h_ref, x1_ref, g_ref, m_ref, w1_ref, w2_ref, *rest, n_ctx_blocks):
    *out_refs, acc_ref, rs_ref = rest
    j = pl.program_id(1)
    sub = MLP_TM // MLP_SPLIT

    def hidden_chunk(first):
        w1b = w1_ref[...].astype(BF16)
        w2b = w2_ref[...].astype(BF16)
        for s in range(MLP_SPLIT):
            rows = slice(s * sub, (s + 1) * sub)
            a = jnp.dot(h_ref[rows, :], w1b, preferred_element_type=F32)
            a = jnp.square(jnp.maximum(a, 0.0)).astype(BF16)
            part = jnp.dot(a, w2b, preferred_element_type=F32)
            for c in range(MLP_NC):
                chunk = part[:, c * MLP_TC:(c + 1) * MLP_TC]
                if first:
                    acc_ref[c, rows, :] = chunk
                else:
                    acc_ref[c, rows, :] += chunk

    pl.when(j == 0)(functools.partial(hidden_chunk, True))
    pl.when((j > 0) & (j < MLP_NH))(functools.partial(hidden_chunk, False))

    @pl.when(j == MLP_NH)
    def _():
        ss = jnp.sum(jnp.square(acc_ref[0]), axis=-1, keepdims=True)
        for c in range(1, MLP_NC):
            ss += jnp.sum(jnp.square(acc_ref[c]), axis=-1, keepdims=True)
        rs_ref[...] = lax.rsqrt(ss * (1.0 / D) + EPS)

    @pl.when(j >= MLP_NH)
    def _():
        f = acc_ref[j - MLP_NH]
        val = x1_ref[...] + m_ref[5:6, :] * ((f * rs_ref[...]) * g_ref[3:4, :])
        if len(out_refs) == 1:
            out_refs[0][...] = val
        else:
            is_ctx = pl.program_id(0) < n_ctx_blocks

            @pl.when(is_ctx)
            def _():
                out_refs[0][...] = val

            @pl.when(jnp.logical_not(is_ctx))
            def _():
                out_refs[1][...] = val


def _mlp(l, h2, x1, norm_g, mods, w1, w2, split_out):
    tm, th, tc = MLP_TM, MLP_TH, MLP_TC
    nc = CTX_ROWS // tm
    hid = lambda j: jnp.minimum(j, MLP_NH - 1)
    col = lambda j: jnp.maximum(j - MLP_NH, 0)
    if split_out:
        out_specs = [
            pl.BlockSpec((tm, tc), lambda i, j: (jnp.minimum(i, nc - 1),
                                                 jnp.where(i < nc, col(j), MLP_NC - 1))),
            pl.BlockSpec((tm, tc), lambda i, j: (jnp.maximum(i - nc, 0),
                                                 jnp.where(i >= nc, col(j), 0))),
        ]
        out_shape = [jax.ShapeDtypeStruct((CTX_ROWS, D), F32), jax.ShapeDtypeStruct((LAT_ROWS, D), F32)]
    else:
        out_specs = pl.BlockSpec((tm, tc), lambda i, j: (i, col(j)))
        out_shape = jax.ShapeDtypeStruct((ROWS, D), F32)
    return pl.pallas_call(
        functools.partial(_mlp_kernel, n_ctx_blocks=nc),
        grid=(ROWS // tm, MLP_NH + MLP_NC),
        in_specs=[
            pl.BlockSpec((tm, D), lambda i, j: (i, 0)),
            pl.BlockSpec((tm, tc), lambda i, j: (i, col(j))),
            pl.BlockSpec((None, 4, tc), lambda i, j: (l, 0, col(j))),
            pl.BlockSpec((None, None, N_MOD, tc), lambda i, j: (l, _row_group(i, tm), 0, col(j))),
            pl.BlockSpec((None, D, th), lambda i, j: (l, 0, hid(j))),
            pl.BlockSpec((None, th, D), lambda i, j: (l, hid(j), 0)),
        ],
        out_specs=out_specs,
        out_shape=out_shape,
        scratch_shapes=[pltpu.VMEM((MLP_NC, tm, tc), F32), pltpu.VMEM((tm, 1), F32)],
        compiler_params=_cp(("arbitrary", "arbitrary")),
        name="mlp",
    )(h2, x1, norm_g, mods, w1, w2)


def _rope_tables():
    t = np.arange(LAT_T)
    rows = (t // GRID_W).astype(np.float32)
    cols = (t % GRID_W).astype(np.float32)
    n = HD // 4
    inv = jnp.asarray(10000.0, F32) ** (-jnp.arange(n, dtype=F32) / n)
    ar = jnp.asarray(rows)[:, None] * inv[None]
    ac = jnp.asarray(cols)[:, None] * inv[None]
    cos = jnp.concatenate([jnp.cos(ar), jnp.cos(ar), jnp.cos(ac), jnp.cos(ac)], axis=-1)
    sin = jnp.concatenate([-jnp.sin(ar), jnp.sin(ar), -jnp.sin(ac), jnp.sin(ac)], axis=-1)
    return jnp.tile(cos, (1, 2)), jnp.tile(sin, (1, 2))


def kernel(x_prompt, x_sample, cache_na_kv, cache_swa_kv, cache_diff_kv, c, c_ctx, ada_w, ada_b,
           norm_g, w_in, conv_w, na_rpb, swa_sink, diff_lambda, diff_norm_g, w_out, mlp_w1, mlp_w2):
    cvec = jnp.concatenate([c_ctx[None], c, jnp.zeros((8 - 1 - N_LAT_B, D), F32)], axis=0)
    mods = _modulation(cvec, ada_w, ada_b).reshape(DEPTH, 8, N_MOD, D)

    cache_na = cache_na_kv.reshape(N_LAT_B, DEPTH, 2, CTX_S, GW)
    cache_swa = cache_swa_kv.reshape(N_LAT_B, DEPTH, 2, CTX_S, LANES)
    cache_diff = cache_diff_kv.reshape(N_LAT_B, DEPTH, 2, CTX_S, GW)
    cos_tab, sin_tab = _rope_tables()
    sink_tab = jnp.broadcast_to(swa_sink[:, :, None], (DEPTH, 8, LANES))
    diff_g = diff_norm_g.reshape(DEPTH, 1, LANES)

    rpb_pad = jnp.pad(na_rpb, ((0, 0), (0, 0), (0, 0), (0, LANES - na_rpb.shape[-1])))

    x = jnp.concatenate([x_prompt.reshape(CTX_ROWS, D), x_sample.reshape(LAT_ROWS, D)], axis=0)
    kv = []
    for l in range(DEPTH):
        lam_init = 0.8 - 0.6 * math.exp(-0.3 * l)
        p = _inproj(l, x, norm_g, mods, w_in)
        mix_ctx, *kv = _ctx_mixer(l, p, conv_w, sink_tab, diff_lambda, diff_g, lam_init, kv)
        o_a = _na(l, p, cache_na, rpb_pad)
        o_b = _conv_latent(l, p, conv_w)
        o_c = _swa(l, p, cache_swa, cos_tab, sin_tab, sink_tab)
        o_d = _diff_latent(l, p, cache_diff, cos_tab, sin_tab, diff_lambda, diff_g, lam_init)
        x1, h2 = _outproj(l, mix_ctx, (o_a, o_b, o_c, o_d), x, norm_g, mods, w_out)
        x = _mlp(l, h2, x1, norm_g, mods, mlp_w1, mlp_w2, split_out=(l == DEPTH - 1))

    y_ctx, y_lat = x
    na_kv, swa_kv, diff_kv = kv
    return (y_ctx.reshape(N_CTX_B, CTX_S, D),
            y_lat.reshape(N_LAT_B, LAT_T, D),
            na_kv.reshape(N_CTX_B, DEPTH, 2, CTX_S, 8, HD),
            swa_kv.reshape(N_CTX_B, DEPTH, 2, CTX_S, 2, HD),
            diff_kv.reshape(N_CTX_B, DEPTH, 2, CTX_S, 4, 2 * HD))
```
